```python
import math
import jax
import jax.numpy as jnp
from jax import lax

D_MODEL = 2048
BATCH = 2
SEQ = 4096
DEPTH = 2
DEC_BATCH = 32
DEC_SEQ = 4
PAST_LEN = 8192
PAGE_SIZE = 128

SB_HEADS = 4
SB_HEAD_DIM = 128
SB_WIDTH = SB_HEADS * SB_HEAD_DIM
SSM_GROUP = 16
SSM_GROUPS = 32
SSM_WIDTH = SSM_GROUPS * SSM_GROUP
SSM_STATE = 64
DT_MIN = 0.001
DT_MAX = 0.1
DA_HEADS = 4
DA_SUB_DIM = 64
DA_QK_WIDTH = DA_HEADS * 2 * DA_SUB_DIM
DA_V_DIM = 128
DA_WIDTH = DA_HEADS * DA_V_DIM
ROPE_THETA = 500000.0
ROPE_DIM = DA_SUB_DIM // 4
N_BRANCH = 3
BRANCH_WIDTH = 512
IN_SIZES = (SB_WIDTH, SB_WIDTH, SB_WIDTH, SSM_WIDTH, DA_QK_WIDTH, DA_QK_WIDTH, DA_WIDTH, N_BRANCH * D_MODEL)
IN_COLS = sum(IN_SIZES)
PEER_HEADS = 8
PEER_KEYS = 128
PEER_EXPERTS = PEER_KEYS * PEER_KEYS
PEER_QDIM = 256
PEER_HALF = PEER_QDIM // 2
PEER_TOPK = 16
PEER_TOKEN_BLOCK = 128
Q_BLOCK = 128
LN_EPS = 1e-5
DEEPNORM_ALPHA = (2 * DEPTH) ** 0.25
DEEPNORM_BETA = (8 * DEPTH) ** -0.25

kernel_name = 'hybrid_stickbreak_s5_diffattn_peer_step'


def _layer_norm(x, g, b):
    xf = x.astype(jnp.float32)
    mu = jnp.mean(xf, axis=-1, keepdims=True)
    var = jnp.mean(jnp.square(xf - mu), axis=-1, keepdims=True)
    return ((xf - mu) * lax.rsqrt(var + LN_EPS) * g + b).astype(x.dtype)


def _split_points():
    pts, acc = [], 0
    for size in IN_SIZES[:-1]:
        acc += size
        pts.append(acc)
    return pts


def _partial_rope(x, pos):
    half = ROPE_DIM // 2
    inv_freq = ROPE_THETA ** (-jnp.arange(half, dtype=jnp.float32) / half)
    ang = pos.astype(jnp.float32)[:, None] * inv_freq[None, :]
    cos = jnp.cos(ang)[:, None, None, :]
    sin = jnp.sin(ang)[:, None, None, :]
    x1 = x[..., :half].astype(jnp.float32)
    x2 = x[..., half:ROPE_DIM].astype(jnp.float32)
    rot = jnp.concatenate([x1 * cos - x2 * sin, x2 * cos + x1 * sin], axis=-1).astype(x.dtype)
    return jnp.concatenate([rot, x[..., ROPE_DIM:]], axis=-1)


def _map_query_blocks(fn, q, q_pos):
    b, tq = q.shape[:2]
    blk = min(Q_BLOCK, tq)
    nblk = -(-tq // blk)
    pad = nblk * blk - tq
    q = jnp.pad(q, [(0, 0), (0, pad)] + [(0, 0)] * (q.ndim - 2))
    q_pos = jnp.pad(q_pos, (0, pad), mode='edge')
    qb = jnp.swapaxes(q.reshape(b, nblk, blk, *q.shape[2:]), 0, 1)
    pb = q_pos.reshape(nblk, blk)
    out = lax.map(lambda a: fn(a[0], a[1]), (qb, pb))
    out = jnp.swapaxes(out, 0, 1).reshape(b, nblk * blk, *out.shape[3:])
    return out[:, :tq]


def _stick_breaking(q, k, v, q_pos, k_pos):
    scale = SB_HEAD_DIM ** -0.5

    def block(qb, pb):
        z = jnp.einsum('bqhd,bkhd->bhqk', qb, k).astype(jnp.float32) * scale
        mask = k_pos[None, :] < pb[:, None]
        log_beta = jax.nn.log_sigmoid(z)
        log_keep = jnp.where(mask, log_beta - z, 0.0)
        later = lax.cumsum(log_keep, axis=3, reverse=True) - log_keep
        w = jnp.where(mask, jnp.exp(log_beta + later), 0.0)
        return jnp.einsum('bhqk,bkhd->bqhd', w.astype(v.dtype), v)

    return _map_query_blocks(block, q, q_pos)


def _diff_attention(q, k, v, lam, q_pos, k_pos):
    scale = DA_SUB_DIM ** -0.5

    def block(qb, pb):
        s = jnp.einsum('bqhcd,bkhcd->bhcqk', qb, k).astype(jnp.float32) * scale
        mask = k_pos[None, :] <= pb[:, None]
        p = jax.nn.softmax(jnp.where(mask, s, -jnp.inf), axis=-1)
        w = p[:, :, 0] - lam[None, :, None, None] * p[:, :, 1]
        return jnp.einsum('bhqk,bkhd->bqhd', w.astype(v.dtype), v)

    return _map_query_blocks(block, q, q_pos)


def _s5(u, h0_re, h0_im, lp):
    f32 = jnp.float32
    bsz, t, _ = u.shape
    ug = u.reshape(bsz, t, SSM_GROUPS, SSM_GROUP).astype(f32)
    lr = lp['ssm_lam_re'].astype(f32)
    li = lp['ssm_lam_im'].astype(f32)
    dt = jnp.exp(lp['ssm_log_dt'].astype(f32))[:, None]
    mag = jnp.exp(lr * dt)
    ab_re = mag * jnp.cos(li * dt)
    ab_im = mag * jnp.sin(li * dt)
    den = lr * lr + li * li
    f_re = ((ab_re - 1.0) * lr + ab_im * li) / den
    f_im = (ab_im * lr - (ab_re - 1.0) * li) / den
    b_re = lp['ssm_b_re'].astype(f32)
    b_im = lp['ssm_b_im'].astype(f32)
    bb_re = f_re[..., None] * b_re - f_im[..., None] * b_im
    bb_im = f_re[..., None] * b_im + f_im[..., None] * b_re
    bu_re = jnp.einsum('gnp,btgp->btgn', bb_re, ug)
    bu_im = jnp.einsum('gnp,btgp->btgn', bb_im, ug)
    h0r = h0_re.astype(f32)
    h0i = h0_im.astype(f32)
    bu_re = bu_re.at[:, 0].add(ab_re * h0r - ab_im * h0i)
    bu_im = bu_im.at[:, 0].add(ab_re * h0i + ab_im * h0r)
    a_re = jnp.broadcast_to(ab_re, bu_re.shape)
    a_im = jnp.broadcast_to(ab_im, bu_im.shape)

    def combine(e1, e2):
        a1r, a1i, b1r, b1i = e1
        a2r, a2i, b2r, b2i = e2
        return (a2r * a1r - a2i * a1i, a2r * a1i + a2i * a1r,
                a2r * b1r - a2i * b1i + b2r, a2r * b1i + a2i * b1r + b2i)

    _, _, hr, hi = lax.associative_scan(combine, (a_re, a_im, bu_re, bu_im), axis=1)
    y = (jnp.einsum('gpn,btgn->btgp', lp['ssm_c_re'].astype(f32), hr)
         - jnp.einsum('gpn,btgn->btgp', lp['ssm_c_im'].astype(f32), hi)
         + lp['ssm_d'].astype(f32) * ug)
    y = jax.nn.gelu(y.reshape(bsz, t, SSM_WIDTH))
    y = y * jax.nn.sigmoid(y @ lp['ssm_w_glu'].astype(f32))
    return y.astype(u.dtype), hr[:, -1], hi[:, -1]


def _peer(x, w_q, sub_keys, u_tab, v_tab):
    bsz, t, d = x.shape
    n = bsz * t
    xf = x.reshape(n, d)
    blk = min(PEER_TOKEN_BLOCK, n)
    nblk = -(-n // blk)
    xf = jnp.pad(xf, ((0, nblk * blk - n), (0, 0)))

    def block(xb):
        q = (xb @ w_q).reshape(blk, PEER_HEADS, 2, PEER_HALF)
        s = jnp.einsum('thcd,hckd->thck', q, sub_keys).astype(jnp.float32)
        s_top, i_top = lax.top_k(s, PEER_TOPK)
        cand = (s_top[:, :, 0, :, None] + s_top[:, :, 1, None, :]).reshape(blk, PEER_HEADS, -1)
        cand_idx = (i_top[:, :, 0, :, None] * PEER_KEYS + i_top[:, :, 1, None, :]).reshape(blk, PEER_HEADS, -1)
        g_s, sel = lax.top_k(cand, PEER_TOPK)
        idx = jnp.take_along_axis(cand_idx, sel, axis=-1)
        g = jax.nn.softmax(g_s, axis=-1)
        u = u_tab[idx]
        h = jax.nn.gelu(jnp.einsum('thed,td->the', u, xb).astype(jnp.float32)) * g
        v = v_tab[idx]
        return jnp.einsum('the,thed->td', h.astype(v.dtype), v)

    out = lax.map(block, xf.reshape(nblk, blk, d))
    return out.reshape(nblk * blk, d)[:n].reshape(bsz, t, d)


def _mix_block(x, pos, lp, past, lambda_init):
    f32 = jnp.float32
    bsz, t, _ = x.shape
    proj = x @ lp['w_in']
    sb_q, sb_k, sb_v, ssm_u, da_q, da_k, da_v, gate_in = jnp.split(proj, _split_points(), axis=-1)
    sb_q = sb_q.reshape(bsz, t, SB_HEADS, SB_HEAD_DIM)
    sb_k = sb_k.reshape(bsz, t, SB_HEADS, SB_HEAD_DIM)
    sb_v = sb_v.reshape(bsz, t, SB_HEADS, SB_HEAD_DIM)
    da_q = _partial_rope(da_q.reshape(bsz, t, DA_HEADS, 2, DA_SUB_DIM), pos)
    da_k = _partial_rope(da_k.reshape(bsz, t, DA_HEADS, 2, DA_SUB_DIM), pos)
    da_v = da_v.reshape(bsz, t, DA_HEADS, DA_V_DIM)

    if past is None:
        k_pos = pos
        sb_k_all, sb_v_all, da_k_all, da_v_all = sb_k, sb_v, da_k, da_v
        h0_re = jnp.zeros((bsz, SSM_GROUPS, SSM_STATE), f32)
        h0_im = jnp.zeros((bsz, SSM_GROUPS, SSM_STATE), f32)
    else:
        p_sb_k, p_sb_v, p_da_k, p_da_v, h0_re, h0_im = past
        k_pos = jnp.concatenate([jnp.arange(p_sb_k.shape[1], dtype=pos.dtype), pos])
        sb_k_all = jnp.concatenate([p_sb_k.astype(sb_k.dtype), sb_k], axis=1)
        sb_v_all = jnp.concatenate([p_sb_v.astype(sb_v.dtype), sb_v], axis=1)
        da_k_all = jnp.concatenate([p_da_k.astype(da_k.dtype), da_k], axis=1)
        da_v_all = jnp.concatenate([p_da_v.astype(da_v.dtype), da_v], axis=1)

    a_out = _stick_breaking(sb_q, sb_k_all, sb_v_all, pos, k_pos).reshape(bsz, t, SB_WIDTH)
    b_out, h_re, h_im = _s5(ssm_u, h0_re, h0_im, lp)
    lam = (jnp.exp(jnp.sum(lp['da_lam_q1'].astype(f32) * lp['da_lam_k1'].astype(f32), axis=-1))
           - jnp.exp(jnp.sum(lp['da_lam_q2'].astype(f32) * lp['da_lam_k2'].astype(f32), axis=-1))
           + lambda_init)
    c = _diff_attention(da_q, da_k_all, da_v_all, lam, pos, k_pos).astype(f32)
    c = c * lax.rsqrt(jnp.mean(c * c, axis=-1, keepdims=True) + LN_EPS) * lp['da_norm_g']
    c_out = ((1.0 - lambda_init) * c).astype(x.dtype).reshape(bsz, t, DA_WIDTH)

    branches = jnp.stack([a_out, b_out, c_out], axis=2)
    proj_b = jnp.einsum('btnc,ncd->btnd', branches, lp['w_branch'])
    gates = jax.nn.sigmoid(gate_in.reshape(bsz, t, N_BRANCH, D_MODEL).astype(f32))
    merged = jnp.sum(gates * proj_b, axis=2).astype(x.dtype)
    out = merged @ lp['w_out']
    return out, (sb_k, sb_v, da_k, da_v, h_re, h_im)


def _trunk_layer(x, pos, lp, past, lambda_init):
    mix, new_state = _mix_block(x, pos, lp, past, lambda_init)
    h = _layer_norm(DEEPNORM_ALPHA * x + mix, lp['ln1_g'], lp['ln1_b'])
    ffn = _peer(h, lp['peer_w_q'], lp['peer_keys'], lp['peer_u'], lp['peer_v'])
    y = _layer_norm(DEEPNORM_ALPHA * h + ffn, lp['ln2_g'], lp['ln2_b'])
    return y, new_state


def _gather_pages(cache, page_table):
    g = cache[page_table]
    return g.reshape(page_table.shape[0], -1, *cache.shape[2:])


def setup_inputs(seed: int = 0) -> dict:
    key = jax.random.key(seed)
    keys = iter(jax.random.split(key, 40))
    f32 = jnp.float32

    def normal(shape, scale):
        return scale * jax.random.normal(next(keys), shape, f32)

    n_pages = PAST_LEN // PAGE_SIZE
    n_pool = (5 * DEC_BATCH * n_pages) // 4
    d_inv = D_MODEL ** -0.5
    x_prompt = normal((BATCH, SEQ, D_MODEL), 1.0)
    x_sample = normal((DEC_BATCH, DEC_SEQ, D_MODEL), 1.0)
    cache_sb_k = normal((DEPTH, n_pool, PAGE_SIZE, SB_HEADS, SB_HEAD_DIM), 1.0)
    cache_sb_v = normal((DEPTH, n_pool, PAGE_SIZE, SB_HEADS, SB_HEAD_DIM), 1.0)
    cache_da_k = normal((DEPTH, n_pool, PAGE_SIZE, DA_HEADS, 2, DA_SUB_DIM), 1.0)
    cache_da_v = normal((DEPTH, n_pool, PAGE_SIZE, DA_HEADS, DA_V_DIM), 1.0)
    state_ssm_re = normal((DEPTH, DEC_BATCH, SSM_GROUPS, SSM_STATE), 0.1)
    state_ssm_im = normal((DEPTH, DEC_BATCH, SSM_GROUPS, SSM_STATE), 0.1)
    page_table = jax.random.permutation(next(keys), n_pool)[:DEC_BATCH * n_pages]
    page_table = page_table.reshape(DEC_BATCH, n_pages).astype(jnp.int32)
    w_in = normal((DEPTH, D_MODEL, IN_COLS), d_inv)
    w_branch = normal((DEPTH, N_BRANCH, BRANCH_WIDTH, D_MODEL), BRANCH_WIDTH ** -0.5 * DEEPNORM_BETA)
    w_out = normal((DEPTH, D_MODEL, D_MODEL), d_inv * DEEPNORM_BETA)
    ssm_lam_re = -0.5 * jnp.exp(normal((DEPTH, SSM_GROUPS, SSM_STATE), 0.01))
    ssm_lam_im = math.pi * jnp.arange(SSM_STATE, dtype=f32) + normal((DEPTH, SSM_GROUPS, SSM_STATE), 0.01)
    ssm_log_dt = jax.random.uniform(next(keys), (DEPTH, SSM_GROUPS), f32, math.log(DT_MIN), math.log(DT_MAX))
    ssm_b_re = normal((DEPTH, SSM_GROUPS, SSM_STATE, SSM_GROUP), (2 * SSM_GROUP) ** -0.5)
    ssm_b_im = normal((DEPTH, SSM_GROUPS, SSM_STATE, SSM_GROUP), (2 * SSM_GROUP) ** -0.5)
    ssm_c_re = normal((DEPTH, SSM_GROUPS, SSM_GROUP, SSM_STATE), (2 * SSM_STATE) ** -0.5)
    ssm_c_im = normal((DEPTH, SSM_GROUPS, SSM_GROUP, SSM_STATE), (2 * SSM_STATE) ** -0.5)
    ssm_d = normal((DEPTH, SSM_GROUPS, SSM_GROUP), 1.0)
    ssm_w_glu = normal((DEPTH, SSM_WIDTH, SSM_WIDTH), SSM_WIDTH ** -0.5)
    da_lam_q1 = normal((DEPTH, DA_HEADS, DA_SUB_DIM), 0.1)
    da_lam_k1 = normal((DEPTH, DA_HEADS, DA_SUB_DIM), 0.1)
    da_lam_q2 = normal((DEPTH, DA_HEADS, DA_SUB_DIM), 0.1)
    da_lam_k2 = normal((DEPTH, DA_HEADS, DA_SUB_DIM), 0.1)
    da_norm_g = 1.0 + normal((DEPTH, DA_V_DIM), 0.02)
    ln1_g = 1.0 + normal((DEPTH, D_MODEL), 0.02)
    ln1_b = normal((DEPTH, D_MODEL), 0.02)
    peer_w_q = normal((DEPTH, D_MODEL, PEER_HEADS * PEER_QDIM), d_inv)
    peer_keys = normal((DEPTH, PEER_HEADS, 2, PEER_KEYS, PEER_HALF), PEER_HALF ** -0.5)
    peer_u = normal((DEPTH, PEER_EXPERTS, D_MODEL), d_inv)
    peer_v = normal((DEPTH, PEER_EXPERTS, D_MODEL), DEEPNORM_BETA)
    ln2_g = 1.0 + normal((DEPTH, D_MODEL), 0.02)
    ln2_b = normal((DEPTH, D_MODEL), 0.02)
    return {'x_prompt': x_prompt, 'x_sample': x_sample,
            'cache_sb_k': cache_sb_k, 'cache_sb_v': cache_sb_v,
            'cache_da_k': cache_da_k, 'cache_da_v': cache_da_v,
            'state_ssm_re': state_ssm_re, 'state_ssm_im': state_ssm_im,
            'page_table': page_table,
            'w_in': w_in, 'w_branch': w_branch, 'w_out': w_out,
            'ssm_lam_re': ssm_lam_re, 'ssm_lam_im': ssm_lam_im, 'ssm_log_dt': ssm_log_dt,
            'ssm_b_re': ssm_b_re, 'ssm_b_im': ssm_b_im, 'ssm_c_re': ssm_c_re, 'ssm_c_im': ssm_c_im,
            'ssm_d': ssm_d, 'ssm_w_glu': ssm_w_glu,
            'da_lam_q1': da_lam_q1, 'da_lam_k1': da_lam_k1, 'da_lam_q2': da_lam_q2, 'da_lam_k2': da_lam_k2,
            'da_norm_g': da_norm_g, 'ln1_g': ln1_g, 'ln1_b': ln1_b,
            'peer_w_q': peer_w_q, 'peer_keys': peer_keys, 'peer_u': peer_u, 'peer_v': peer_v,
            'ln2_g': ln2_g, 'ln2_b': ln2_b}


def reference(x_prompt, x_sample, cache_sb_k, cache_sb_v, cache_da_k, cache_da_v,
              state_ssm_re, state_ssm_im, page_table,
              w_in, w_branch, w_out,
              ssm_lam_re, ssm_lam_im, ssm_log_dt, ssm_b_re, ssm_b_im, ssm_c_re, ssm_c_im,
              ssm_d, ssm_w_glu,
              da_lam_q1, da_lam_k1, da_lam_q2, da_lam_k2, da_norm_g,
              ln1_g, ln1_b, peer_w_q, peer_keys, peer_u, peer_v, ln2_g, ln2_b):
    n_past = page_table.shape[1] * PAGE_SIZE
    pos_prompt = jnp.arange(x_prompt.shape[1], dtype=jnp.int32)
    pos_sample = n_past + jnp.arange(x_sample.shape[1], dtype=jnp.int32)
    y_prompt, y_sample = x_prompt, x_sample
    rows_prompt, rows_sample = [], []
    for l in range(DEPTH):
        lp = {'w_in': w_in[l], 'w_branch': w_branch[l], 'w_out': w_out[l],
              'ssm_lam_re': ssm_lam_re[l], 'ssm_lam_im': ssm_lam_im[l], 'ssm_log_dt': ssm_log_dt[l],
              'ssm_b_re': ssm_b_re[l], 'ssm_b_im': ssm_b_im[l],
              'ssm_c_re': ssm_c_re[l], 'ssm_c_im': ssm_c_im[l],
              'ssm_d': ssm_d[l], 'ssm_w_glu': ssm_w_glu[l],
              'da_lam_q1': da_lam_q1[l], 'da_lam_k1': da_lam_k1[l],
              'da_lam_q2': da_lam_q2[l], 'da_lam_k2': da_lam_k2[l], 'da_norm_g': da_norm_g[l],
              'ln1_g': ln1_g[l], 'ln1_b': ln1_b[l],
              'peer_w_q': peer_w_q[l], 'peer_keys': peer_keys[l], 'peer_u': peer_u[l], 'peer_v': peer_v[l],
              'ln2_g': ln2_g[l], 'ln2_b': ln2_b[l]}
        lambda_init = 0.8 - 0.6 * math.exp(-0.3 * l)
        y_prompt, st_p = _trunk_layer(y_prompt, pos_prompt, lp, None, lambda_init)
        past = (_gather_pages(cache_sb_k[l], page_table), _gather_pages(cache_sb_v[l], page_table),
                _gather_pages(cache_da_k[l], page_table), _gather_pages(cache_da_v[l], page_table),
                state_ssm_re[l], state_ssm_im[l])
        y_sample, st_s = _trunk_layer(y_sample, pos_sample, lp, past, lambda_init)
        rows_prompt.append(st_p)
        rows_sample.append(st_s)
    sb_k_p, sb_v_p, da_k_p, da_v_p, ssm_re_p, ssm_im_p = [jnp.stack(z) for z in zip(*rows_prompt)]
    sb_k_s, sb_v_s, da_k_s, da_v_s, ssm_re_s, ssm_im_s = [jnp.stack(z) for z in zip(*rows_sample)]
    return (y_prompt, y_sample,
            sb_k_p, sb_v_p, da_k_p, da_v_p, ssm_re_p, ssm_im_p,
            sb_k_s, sb_v_s, da_k_s, da_v_s, ssm_re_s, ssm_im_s)
```

```python
import functools
import math

import jax
import jax.numpy as jnp
from jax import lax
from jax.experimental import pallas as pl
from jax.experimental.pallas import tpu as pltpu

F32 = jnp.float32
BF16 = jnp.bfloat16

D_MODEL = 2048
PAGE_SIZE = 128
SB_HEADS = 4
SB_HEAD_DIM = 128
SSM_GROUP = 16
SSM_GROUPS = 32
SSM_STATE = 64
SSM_WIDTH = SSM_GROUPS * SSM_GROUP
SSM_LANES = SSM_GROUPS * SSM_STATE
DA_HEADS = 4
DA_SUB_DIM = 64
DA_V_DIM = 128
ROPE_THETA = 500000.0
ROPE_DIM = DA_SUB_DIM // 4
N_BRANCH = 3
BRANCH_WIDTH = 512
N_PROJ_GROUPS = 7
GATE_COL0 = N_PROJ_GROUPS * BRANCH_WIDTH
PEER_HEADS = 8
PEER_KEYS = 128
PEER_HALF = 128
PEER_TOPK = 16
LN_EPS = 1e-5

V7X_VMEM_BYTES = 64 * 1024 * 1024
VMEM_LIMIT = 56 * 1024 * 1024

NT_DIMS = (((1,), (1,)), ((), ()))


def _params(*sem):
    return pltpu.CompilerParams(dimension_semantics=sem, vmem_limit_bytes=VMEM_LIMIT)


def _gelu_tanh(x):
    return 0.5 * x * (1.0 + jnp.tanh(math.sqrt(2.0 / math.pi) * (x + 0.044715 * (x * x * x))))


def _sigmoid(x):
    return 1.0 / (1.0 + jnp.exp(-x))


def _proj_kernel(x_ref, w_ref, cos_ref, sa_ref, sb_ref, o_ref):
    j = pl.program_id(0)
    o_ref[0] = jnp.dot(x_ref[...], w_ref[...], preferred_element_type=F32)

    @pl.when((j == 4) | (j == 5))
    def _():
        c = cos_ref[...]
        sa = sa_ref[...]
        sb = sb_ref[...]
        for t in range(BRANCH_WIDTH // 128):
            xt = o_ref[0, :, t * 128:(t + 1) * 128]
            o_ref[0, :, t * 128:(t + 1) * 128] = (
                xt * c + pltpu.roll(xt, 8, 1) * sa + pltpu.roll(xt, 120, 1) * sb)


def _project(x_bf, w_in_bf, rope_tabs, tm):
    m = x_bf.shape[0]
    nr = m // tm
    cos_t, sa_t, sb_t = rope_tabs

    def tab_map(j, i):
        return (jnp.where((j == 4) | (j == 5), i, 0), 0)

    return pl.pallas_call(
        _proj_kernel,
        grid=(N_PROJ_GROUPS, nr),
        in_specs=[
            pl.BlockSpec((tm, D_MODEL), lambda j, i: (i, 0)),
            pl.BlockSpec((D_MODEL, BRANCH_WIDTH), lambda j, i: (0, j)),
            pl.BlockSpec((tm, 128), tab_map),
            pl.BlockSpec((tm, 128), tab_map),
            pl.BlockSpec((tm, 128), tab_map),
        ],
        out_specs=pl.BlockSpec((1, tm, BRANCH_WIDTH), lambda j, i: (j, i, 0)),
        out_shape=jax.ShapeDtypeStruct((N_PROJ_GROUPS, m, BRANCH_WIDTH), F32),
        compiler_params=_params("arbitrary", "arbitrary"),
        name="proj",
    )(x_bf, w_in_bf, cos_t, sa_t, sb_t)


def _rope_tables(pos):
    half = ROPE_DIM // 2
    inv_freq = ROPE_THETA ** (-jnp.arange(half, dtype=F32) / half)
    ang = pos.astype(F32)[:, None] * inv_freq[None, :]
    cos = jnp.cos(ang)
    sin = jnp.sin(ang)
    n = pos.shape[0]
    ones = jnp.ones((n, DA_SUB_DIM - ROPE_DIM), F32)
    zeros = jnp.zeros((n, DA_SUB_DIM - ROPE_DIM), F32)
    zh = jnp.zeros((n, half), F32)
    c64 = jnp.concatenate([cos, cos, ones], axis=1)
    sa64 = jnp.concatenate([zh, sin, zeros], axis=1)
    sb64 = jnp.concatenate([-sin, zh, zeros], axis=1)
    return tuple(jnp.concatenate([t, t], axis=1) for t in (c64, sa64, sb64))


def _log_sigmoid(z):
    return jnp.minimum(z, 0.0) - jnp.log1p(jnp.exp(-jnp.abs(z)))


def _split_bf16(x):
    hi = x.astype(BF16)
    lo = (x - hi.astype(F32)).astype(BF16)
    return hi, lo


def _sb_block(q, kb, vb, tri, c, acc, scale, mask):
    tk = kb.shape[0]
    z = lax.dot_general(q, kb, NT_DIMS, preferred_element_type=F32) * scale
    lb = _log_sigmoid(z)
    lk = lb - z
    if mask is not None:
        lk = jnp.where(mask, lk, 0.0)
    hi, lo = _split_bf16(lk)
    r = (jnp.dot(hi, tri, preferred_element_type=F32)
         + jnp.dot(lo, tri, preferred_element_type=F32))
    w = jnp.exp(lb + r[:, :tk] + c)
    if mask is not None:
        w = jnp.where(mask, w, 0.0)
    acc = acc + jnp.dot(w.astype(BF16), vb, preferred_element_type=F32)
    return c + r[:, tk:], acc


def _sb_prompt_kernel(q_ref, k_ref, v_ref, tri_ref, o_ref, kbf, vbf, *, blk, scale):
    qi = pl.program_id(2)

    @pl.when(qi == 0)
    def _():
        kbf[...] = k_ref[0].astype(BF16)
        vbf[...] = v_ref[0].astype(BF16)

    q = q_ref[0].astype(BF16)
    tri = tri_ref[...]
    row = lax.broadcasted_iota(jnp.int32, (blk, blk), 0)
    col = lax.broadcasted_iota(jnp.int32, (blk, blk), 1)
    off = pl.multiple_of(qi * blk, blk)
    c0 = jnp.zeros((blk, blk), F32)
    a0 = jnp.zeros((blk, SB_HEAD_DIM), F32)
    c, acc = _sb_block(q, kbf[pl.ds(off, blk), :], vbf[pl.ds(off, blk), :], tri, c0, a0,
                       scale, col < row)

    def body(jj, carry):
        o = pl.multiple_of((qi - 1 - jj) * blk, blk)
        return _sb_block(q, kbf[pl.ds(o, blk), :], vbf[pl.ds(o, blk), :], tri,
                         carry[0], carry[1], scale, None)

    c, acc = lax.fori_loop(0, qi, body, (c, acc))
    o_ref[...] = acc


def _tri_ones(blk):
    j = jnp.arange(blk)[:, None]
    s = jnp.arange(blk)[None, :]
    return jnp.concatenate([(j > s).astype(BF16), jnp.ones((blk, blk), BF16)], axis=1)


def _sb_prompt(proj, bsz, t, blk):
    nq = t // blk
    kern = functools.partial(_sb_prompt_kernel, blk=blk, scale=SB_HEAD_DIM ** -0.5)
    return pl.pallas_call(
        kern,
        grid=(bsz, SB_HEADS, nq),
        in_specs=[
            pl.BlockSpec((1, blk, SB_HEAD_DIM), lambda b, h, i: (0, b * nq + i, h)),
            pl.BlockSpec((1, t, SB_HEAD_DIM), lambda b, h, i: (1, b, h)),
            pl.BlockSpec((1, t, SB_HEAD_DIM), lambda b, h, i: (2, b, h)),
            pl.BlockSpec((blk, 2 * blk), lambda b, h, i: (0, 0)),
        ],
        out_specs=pl.BlockSpec((blk, SB_HEAD_DIM), lambda b, h, i: (b * nq + i, h)),
        out_shape=jax.ShapeDtypeStruct((bsz * t, SB_HEADS * SB_HEAD_DIM), F32),
        scratch_shapes=[pltpu.VMEM((t, SB_HEAD_DIM), BF16), pltpu.VMEM((t, SB_HEAD_DIM), BF16)],
        compiler_params=_params("arbitrary", "arbitrary", "arbitrary"),
        name="sb_prompt",
    )(proj, proj, proj, _tri_ones(blk))


def _da_lambda(lq1, lk1, lq2, lk2, h, lambda_init):
    a = jnp.sum(lq1[pl.ds(h, 1), :] * lk1[pl.ds(h, 1), :], axis=1, keepdims=True)
    b = jnp.sum(lq2[pl.ds(h, 1), :] * lk2[pl.ds(h, 1), :], axis=1, keepdims=True)
    return jnp.exp(a) - jnp.exp(b) + lambda_init


def _split_components(q):
    lane = lax.broadcasted_iota(jnp.int32, q.shape, 1)
    q0 = jnp.where(lane < DA_SUB_DIM, q, 0.0).astype(BF16)
    q1 = jnp.where(lane >= DA_SUB_DIM, q, 0.0).astype(BF16)
    return q0, q1


def _softmax_block(qc, kb, vb, m, l, acc, scale, mask):
    s = lax.dot_general(qc, kb, NT_DIMS, preferred_element_type=F32) * scale
    if mask is not None:
        s = jnp.where(mask, s, -jnp.inf)
    m_new = jnp.maximum(m, jnp.max(s, axis=1, keepdims=True))
    p = jnp.exp(s - m_new)
    alpha = jnp.exp(m - m_new)
    l = alpha * l + jnp.sum(p, axis=1, keepdims=True)
    acc = alpha * acc + jnp.dot(p.astype(BF16), vb, preferred_element_type=F32)
    return m_new, l, acc


def _da_finish(st0, st1, lam, g, lambda_init):
    o = st0[2] / st0[1] - lam * (st1[2] / st1[1])
    o = o * lax.rsqrt(jnp.mean(o * o, axis=-1, keepdims=True) + LN_EPS) * g
    return (1.0 - lambda_init) * o


def _da_prompt_kernel(q_ref, k_ref, v_ref, lq1, lk1, lq2, lk2, g_ref, o_ref, kbf, vbf,
                      *, blk, scale, lambda_init):
    h = pl.program_id(1)
    qi = pl.program_id(2)

    @pl.when(qi == 0)
    def _():
        kbf[...] = k_ref[0].astype(BF16)
        vbf[...] = v_ref[0].astype(BF16)

    q0, q1 = _split_components(q_ref[0])
    row = lax.broadcasted_iota(jnp.int32, (blk, blk), 0)
    col = lax.broadcasted_iota(jnp.int32, (blk, blk), 1)
    mask = col <= row
    off = pl.multiple_of(qi * blk, blk)
    init = (jnp.full((blk, 1), -jnp.inf, F32), jnp.zeros((blk, 1), F32),
            jnp.zeros((blk, DA_V_DIM), F32))
    kd = kbf[pl.ds(off, blk), :]
    vd = vbf[pl.ds(off, blk), :]
    st0 = _softmax_block(q0, kd, vd, *init, scale, mask)
    st1 = _softmax_block(q1, kd, vd, *init, scale, mask)

    def body(j, carry):
        o = pl.multiple_of(j * blk, blk)
        kb = kbf[pl.ds(o, blk), :]
        vb = vbf[pl.ds(o, blk), :]
        return (_softmax_block(q0, kb, vb, *carry[0], scale, None),
                _softmax_block(q1, kb, vb, *carry[1], scale, None))

    st0, st1 = lax.fori_loop(0, qi, body, (st0, st1))
    lam = _da_lambda(lq1, lk1, lq2, lk2, h, lambda_init)
    o_ref[...] = _da_finish(st0, st1, lam, g_ref[...], lambda_init)


def _da_prompt(proj, lp, bsz, t, blk, lambda_init):
    nq = t // blk
    kern = functools.partial(_da_prompt_kernel, blk=blk, scale=DA_SUB_DIM ** -0.5,
                             lambda_init=lambda_init)
    lam_spec = pl.BlockSpec((DA_HEADS, DA_SUB_DIM), lambda b, h, i: (0, 0))
    return pl.pallas_call(
        kern,
        grid=(bsz, DA_HEADS, nq),
        in_specs=[
            pl.BlockSpec((1, blk, 128), lambda b, h, i: (4, b * nq + i, h)),
            pl.BlockSpec((1, t, 128), lambda b, h, i: (5, b, h)),
            pl.BlockSpec((1, t, DA_V_DIM), lambda b, h, i: (6, b, h)),
            lam_spec, lam_spec, lam_spec, lam_spec,
            pl.BlockSpec((1, DA_V_DIM), lambda b, h, i: (0, 0)),
        ],
        out_specs=pl.BlockSpec((blk, DA_V_DIM), lambda b, h, i: (b * nq + i, h)),
        out_shape=jax.ShapeDtypeStruct((bsz * t, DA_HEADS * DA_V_DIM), F32),
        scratch_shapes=[pltpu.VMEM((t, 128), BF16), pltpu.VMEM((t, DA_V_DIM), BF16)],
        compiler_params=_params("arbitrary", "arbitrary", "arbitrary"),
        name="da_prompt",
    )(proj, proj, proj, lp['da_lam_q1'], lp['da_lam_k1'], lp['da_lam_q2'], lp['da_lam_k2'],
      lp['da_norm_g'].reshape(1, DA_V_DIM))


SAMPLE_ROWS = 8


def _sample_attn_kernel(pt_ref, q_ref, nk_ref, nv_ref, dq_ref, dnk_ref, dnv_ref,
                        ck_ref, cv_ref, dck_ref, dcv_ref, tri_ref,
                        lq1, lk1, lq2, lk2, g_ref,
                        oa_ref, oc_ref,
                        c_s, acc_s, m_s, l_s, dacc_s,
                        *, n_new, lambda_init):
    del pt_ref
    p = pl.program_id(1)
    n_pages = pl.num_programs(1)
    rows = SAMPLE_ROWS
    sb_scale = SB_HEAD_DIM ** -0.5
    da_scale = DA_SUB_DIM ** -0.5
    tri = tri_ref[...]
    pad = jnp.zeros((PAGE_SIZE - rows, 512), F32)

    def sb_all(kpage, vpage, mask, first):
        for h in range(SB_HEADS):
            sl = slice(h * 128, (h + 1) * 128)
            q = q_ref[0, 0, :, sl].astype(BF16)
            c = jnp.zeros((rows, PAGE_SIZE), F32) if first else c_s[:, sl]
            a = jnp.zeros((rows, SB_HEAD_DIM), F32) if first else acc_s[:, sl]
            c, a = _sb_block(q, kpage[:, sl].astype(BF16), vpage[:, sl].astype(BF16), tri,
                             c, a, sb_scale, mask)
            c_s[:, sl] = c
            acc_s[:, sl] = a

    def da_all(kpage, vpage, mask, first):
        for h in range(DA_HEADS):
            sl = slice(h * 128, (h + 1) * 128)
            q0, q1 = _split_components(dq_ref[0, 0, :, sl])
            kb = kpage[:, sl].astype(BF16)
            vb = vpage[:, sl].astype(BF16)
            for c, qc in enumerate((q0, q1)):
                if first:
                    st = (jnp.full((rows, 1), -jnp.inf, F32), jnp.zeros((rows, 1), F32),
                          jnp.zeros((rows, DA_V_DIM), F32))
                else:
                    st = (m_s[c, :, sl][:, :1], l_s[c, :, sl][:, :1], dacc_s[c, :, sl])
                m, l, a = _softmax_block(qc, kb, vb, *st, da_scale, mask)
                m_s[c, :, sl] = jnp.broadcast_to(m, (rows, 128))
                l_s[c, :, sl] = jnp.broadcast_to(l, (rows, 128))
                dacc_s[c, :, sl] = a

    @pl.when(p == 0)
    def _():
        row = lax.broadcasted_iota(jnp.int32, (rows, PAGE_SIZE), 0)
        col = lax.broadcasted_iota(jnp.int32, (rows, PAGE_SIZE), 1)
        valid = col < n_new
        sb_all(jnp.concatenate([nk_ref[0, 0], pad], axis=0),
               jnp.concatenate([nv_ref[0, 0], pad], axis=0), valid & (col < row), True)
        da_all(jnp.concatenate([dnk_ref[0, 0], pad], axis=0),
               jnp.concatenate([dnv_ref[0, 0], pad], axis=0), valid & (col <= row), True)

    sb_all(ck_ref[...], cv_ref[...], None, False)
    da_all(dck_ref[...], dcv_ref[...], None, False)

    @pl.when(p == n_pages - 1)
    def _():
        oa_ref[0] = acc_s[...]
        for h in range(DA_HEADS):
            sl = slice(h * 128, (h + 1) * 128)
            lam = _da_lambda(lq1, lk1, lq2, lk2, h, lambda_init)
            st0 = (None, l_s[0, :, sl][:, :1], dacc_s[0, :, sl])
            st1 = (None, l_s[1, :, sl][:, :1], dacc_s[1, :, sl])
            oc_ref[0, :, sl] = _da_finish(st0, st1, lam, g_ref[...], lambda_init)


def _sample_attn(proj_s, caches, layer, page_table, lp, n_new, lambda_init):
    nb, n_pages = page_table.shape
    ck, cv, dck, dcv = caches
    kern = functools.partial(_sample_attn_kernel, n_new=n_new, lambda_init=lambda_init)

    def new_spec(g):
        return pl.BlockSpec((1, 1, SAMPLE_ROWS, 512), lambda b, p, pt: (g, b, 0, 0))

    cache_spec = pl.BlockSpec((None, None, PAGE_SIZE, 512),
                              lambda b, p, pt: (layer, pt[b, n_pages - 1 - p], 0, 0))
    lam_spec = pl.BlockSpec((DA_HEADS, DA_SUB_DIM), lambda b, p, pt: (0, 0))
    out_spec = pl.BlockSpec((1, SAMPLE_ROWS, 512), lambda b, p, pt: (b, 0, 0))
    grid_spec = pltpu.PrefetchScalarGridSpec(
        num_scalar_prefetch=1,
        grid=(nb, n_pages),
        in_specs=[new_spec(0), new_spec(1), new_spec(2), new_spec(4), new_spec(5), new_spec(6),
                  cache_spec, cache_spec, cache_spec, cache_spec,
                  pl.BlockSpec((PAGE_SIZE, 2 * PAGE_SIZE), lambda b, p, pt: (0, 0)),
                  lam_spec, lam_spec, lam_spec, lam_spec,
                  pl.BlockSpec((1, DA_V_DIM), lambda b, p, pt: (0, 0))],
        out_specs=[out_spec, out_spec],
        scratch_shapes=[pltpu.VMEM((SAMPLE_ROWS, 512), F32), pltpu.VMEM((SAMPLE_ROWS, 512), F32),
                        pltpu.VMEM((2, SAMPLE_ROWS, 512), F32), pltpu.VMEM((2, SAMPLE_ROWS, 512), F32),
                        pltpu.VMEM((2, SAMPLE_ROWS, 512), F32)],
    )
    shape = jax.ShapeDtypeStruct((nb, SAMPLE_ROWS, 512), F32)
    return pl.pallas_call(
        kern, grid_spec=grid_spec, out_shape=[shape, shape],
        compiler_params=_params("arbitrary", "arbitrary"),
        name="sample_attn",
    )(page_table, proj_s, proj_s, proj_s, proj_s, proj_s, proj_s, ck, cv, dck, dcv,
      _tri_ones(PAGE_SIZE), lp['da_lam_q1'], lp['da_lam_k1'], lp['da_lam_q2'], lp['da_lam_k2'],
      lp['da_norm_g'].reshape(1, DA_V_DIM))


LANE_GROUP = 512


def _s5_kernel(u_ref, bb_ref, a_ref, h0_ref, cc_ref, d_ref, glu_ref, y_ref, hf_ref,
               bu_s, st_s, *, nseq, steps, emit):
    i = pl.program_id(0)
    n = SSM_LANES

    @pl.when(i == 0)
    def _():
        st_s[...] = h0_ref[...]

    u = u_ref[...]
    bu_s[...] = jnp.dot(u.astype(BF16), bb_ref[...], preferred_element_type=F32)

    for sg in range(nseq // 8):
        for lg in range(n // LANE_GROUP):
            lre = slice(lg * LANE_GROUP, (lg + 1) * LANE_GROUP)
            lim = slice(n + lg * LANE_GROUP, n + (lg + 1) * LANE_GROUP)
            rs = slice(sg * 8, (sg + 1) * 8)
            ar = jnp.broadcast_to(a_ref[0:1, lre], (8, LANE_GROUP))
            ai = jnp.broadcast_to(a_ref[0:1, lim], (8, LANE_GROUP))

            def step(t, carry, lre=lre, lim=lim, sg=sg, ar=ar, ai=ai):
                hr, hi = carry
                r0 = pl.multiple_of(t * nseq + sg * 8, 8)
                nr = ar * hr - ai * hi + bu_s[pl.ds(r0, 8), lre]
                ni = ar * hi + ai * hr + bu_s[pl.ds(r0, 8), lim]
                if emit:
                    bu_s[pl.ds(r0, 8), lre] = nr
                    bu_s[pl.ds(r0, 8), lim] = ni
                return nr, ni

            hr, hi = lax.fori_loop(0, steps, step, (st_s[rs, lre], st_s[rs, lim]))
            st_s[rs, lre] = hr
            st_s[rs, lim] = hi

    if emit:
        y = jnp.dot(bu_s[...].astype(BF16), cc_ref[...], preferred_element_type=F32) + d_ref[...] * u
        y = _gelu_tanh(y)
        gate = jnp.dot(y.astype(BF16), glu_ref[...], preferred_element_type=F32)
        y_ref[...] = y * _sigmoid(gate)
    else:
        y_ref[...] = jnp.zeros(y_ref.shape, F32)

    hf_ref[...] = st_s[...]


def _s5_scan(u_rows, ssm, h0, nseq, steps, emit):
    rows = u_rows.shape[0]
    blk_rows = steps * nseq
    n_chunks = rows // blk_rows
    kern = functools.partial(_s5_kernel, nseq=nseq, steps=steps, emit=emit)
    const = lambda i: (0, 0)
    y_rows = blk_rows if emit else 8
    y, hf = pl.pallas_call(
        kern,
        grid=(n_chunks,),
        in_specs=[
            pl.BlockSpec((blk_rows, SSM_WIDTH), lambda i: (i, 0)),
            pl.BlockSpec((SSM_WIDTH, 2 * SSM_LANES), const),
            pl.BlockSpec((1, 2 * SSM_LANES), const),
            pl.BlockSpec((nseq, 2 * SSM_LANES), const),
            pl.BlockSpec((2 * SSM_LANES, SSM_WIDTH), const),
            pl.BlockSpec((1, SSM_WIDTH), const),
            pl.BlockSpec((SSM_WIDTH, SSM_WIDTH), const),
        ],
        out_specs=[pl.BlockSpec((y_rows, SSM_WIDTH), (lambda i: (i, 0)) if emit else const),
                   pl.BlockSpec((nseq, 2 * SSM_LANES), const)],
        out_shape=[jax.ShapeDtypeStruct((rows if emit else 8, SSM_WIDTH), F32),
                   jax.ShapeDtypeStruct((nseq, 2 * SSM_LANES), F32)],
        scratch_shapes=[pltpu.VMEM((blk_rows, 2 * SSM_LANES), F32),
                        pltpu.VMEM((nseq, 2 * SSM_LANES), F32)],
        compiler_params=_params("arbitrary"),
        name="s5_emit" if emit else "s5_state",
    )(u_rows, ssm['bb'], ssm['a'], h0, ssm['cc'], ssm['d'], ssm['glu'])
    return y, hf


def _segment_init_kernel(a_ref, f_ref, o_ref, *, seg_len, nb, nseg):
    n = SSM_LANES
    pr = a_ref[:, :n]
    pi = a_ref[:, n:]
    rr, ri = None, None
    e = seg_len
    while e:
        if e & 1:
            if rr is None:
                rr, ri = pr, pi
            else:
                rr, ri = rr * pr - ri * pi, rr * pi + ri * pr
        e >>= 1
        if e:
            pr, pi = pr * pr - pi * pi, 2.0 * pr * pi
    for b in range(nb):
        hr = jnp.zeros((1, n), F32)
        hi = jnp.zeros((1, n), F32)
        for k in range(nseg):
            r = b * nseg + k
            o_ref[r:r + 1, :n] = hr
            o_ref[r:r + 1, n:] = hi
            fr = f_ref[r:r + 1, :n]
            fi = f_ref[r:r + 1, n:]
            hr, hi = rr * hr - ri * hi + fr, rr * hi + ri * hr + fi


def _segment_init(a, f, seg_len, nb, nseg):
    kern = functools.partial(_segment_init_kernel, seg_len=seg_len, nb=nb, nseg=nseg)
    return pl.pallas_call(
        kern, out_shape=jax.ShapeDtypeStruct(f.shape, F32), name="s5_segments",
        compiler_params=pltpu.CompilerParams(vmem_limit_bytes=VMEM_LIMIT),
    )(a, f)


def _ssm_params(lp):
    lr = lp['ssm_lam_re']
    li = lp['ssm_lam_im']
    dt = jnp.exp(lp['ssm_log_dt'])[:, None]
    mag = jnp.exp(lr * dt)
    ab_re = mag * jnp.cos(li * dt)
    ab_im = mag * jnp.sin(li * dt)
    den = lr * lr + li * li
    f_re = ((ab_re - 1.0) * lr + ab_im * li) / den
    f_im = (ab_im * lr - (ab_re - 1.0) * li) / den
    b_re = lp['ssm_b_re']
    b_im = lp['ssm_b_im']
    bb_re = f_re[..., None] * b_re - f_im[..., None] * b_im
    bb_im = f_re[..., None] * b_im + f_im[..., None] * b_re
    eye = jnp.eye(SSM_GROUPS, dtype=F32)

    def in_blockdiag(w):
        return jnp.einsum('gnp,gh->gphn', w, eye).reshape(SSM_WIDTH, SSM_LANES)

    def out_blockdiag(w):
        return jnp.einsum('gpn,gh->gnhp', w, eye).reshape(SSM_LANES, SSM_WIDTH)

    return {
        'bb': jnp.concatenate([in_blockdiag(bb_re), in_blockdiag(bb_im)], axis=1).astype(BF16),
        'cc': jnp.concatenate([out_blockdiag(lp['ssm_c_re']), -out_blockdiag(lp['ssm_c_im'])],
                              axis=0).astype(BF16),
        'a': jnp.concatenate([ab_re.reshape(1, SSM_LANES), ab_im.reshape(1, SSM_LANES)], axis=1),
        'd': lp['ssm_d'].reshape(1, SSM_WIDTH),
        'glu': lp['ssm_w_glu'].astype(BF16),
    }


def _s5_prompt(u, ssm, bsz, t, nseg, steps):
    nseq = bsz * nseg
    seg_len = t // nseg
    u_rows = u.reshape(bsz, nseg, seg_len, SSM_WIDTH).transpose(2, 0, 1, 3).reshape(seg_len * nseq, SSM_WIDTH)
    zero = jnp.zeros((nseq, 2 * SSM_LANES), F32)
    _, f = _s5_scan(u_rows, ssm, zero, nseq, steps, emit=False)
    h0 = _segment_init(ssm['a'], f, seg_len, bsz, nseg)
    y_rows, hf = _s5_scan(u_rows, ssm, h0, nseq, steps, emit=True)
    y = y_rows.reshape(seg_len, bsz, nseg, SSM_WIDTH).transpose(1, 2, 0, 3).reshape(bsz * t, SSM_WIDTH)
    last = hf.reshape(bsz, nseg, 2 * SSM_LANES)[:, -1]
    return y, last[:, :SSM_LANES], last[:, SSM_LANES:]


def _s5_sample(u, ssm, h0_re, h0_im, nb, t):
    u_rows = u.reshape(nb, t, SSM_WIDTH).transpose(1, 0, 2).reshape(t * nb, SSM_WIDTH)
    h0 = jnp.concatenate([h0_re.reshape(nb, SSM_LANES), h0_im.reshape(nb, SSM_LANES)], axis=1)
    y_rows, hf = _s5_scan(u_rows, ssm, h0, nb, t, emit=True)
    y = y_rows.reshape(t, nb, SSM_WIDTH).transpose(1, 0, 2).reshape(nb * t, SSM_WIDTH)
    return y, hf[:, :SSM_LANES], hf[:, SSM_LANES:]


def _merge_kernel(x_ref, a_ref, b_ref, c_ref, wg0, wg1, wg2, wb_ref, o_ref):
    x = x_ref[...]
    acc = None
    for n, (br, wg) in enumerate(((a_ref, wg0), (b_ref, wg1), (c_ref, wg2))):
        gate = _sigmoid(jnp.dot(x, wg[...], preferred_element_type=F32))
        pb = jnp.dot(br[...].astype(BF16), wb_ref[n], preferred_element_type=F32)
        acc = gate * pb if acc is None else acc + gate * pb
    o_ref[...] = acc.astype(o_ref.dtype)


def _merge(x_bf, a_out, b_out, c_out, w_in_bf, w_branch_bf, tm, tn):
    m = x_bf.shape[0]
    ncol = D_MODEL // tn
    g0 = GATE_COL0 // tn

    def gate_spec(n):
        return pl.BlockSpec((D_MODEL, tn), lambda i, j: (0, g0 + n * ncol + j))

    br_spec = pl.BlockSpec((tm, BRANCH_WIDTH), lambda i, j: (i, 0))
    return pl.pallas_call(
        _merge_kernel,
        grid=(m // tm, ncol),
        in_specs=[pl.BlockSpec((tm, D_MODEL), lambda i, j: (i, 0)), br_spec, br_spec, br_spec,
                  gate_spec(0), gate_spec(1), gate_spec(2),
                  pl.BlockSpec((N_BRANCH, BRANCH_WIDTH, tn), lambda i, j: (0, 0, j))],
        out_specs=pl.BlockSpec((tm, tn), lambda i, j: (i, j)),
        out_shape=jax.ShapeDtypeStruct((m, D_MODEL), BF16),
        compiler_params=_params("arbitrary", "arbitrary"),
        name="merge",
    )(x_bf, a_out, b_out, c_out, w_in_bf, w_in_bf, w_in_bf, w_branch_bf)


def _layer_norm(v, g, b):
    mu = jnp.mean(v, axis=-1, keepdims=True)
    d = v - mu
    var = jnp.mean(d * d, axis=-1, keepdims=True)
    return d * lax.rsqrt(var + LN_EPS) * g + b


def _out_ln_kernel(m_ref, w_ref, x_ref, g_ref, b_ref, h_ref, hb_ref, *, alpha):
    mix = jnp.dot(m_ref[...], w_ref[...], preferred_element_type=F32)
    h = _layer_norm(alpha * x_ref[...] + mix, g_ref[...], b_ref[...])
    h_ref[...] = h
    hb_ref[...] = h.astype(BF16)


def _out_ln(merged_bf, w_out_bf, x, g, b, alpha, tm):
    m = x.shape[0]
    row = pl.BlockSpec((tm, D_MODEL), lambda i: (i, 0))
    vec = pl.BlockSpec((1, D_MODEL), lambda i: (0, 0))
    return pl.pallas_call(
        functools.partial(_out_ln_kernel, alpha=alpha),
        grid=(m // tm,),
        in_specs=[row, pl.BlockSpec((D_MODEL, D_MODEL), lambda i: (0, 0)), row, vec, vec],
        out_specs=[row, row],
        out_shape=[jax.ShapeDtypeStruct((m, D_MODEL), F32), jax.ShapeDtypeStruct((m, D_MODEL), BF16)],
        compiler_params=_params("arbitrary"),
        name="out_ln",
    )(merged_bf, w_out_bf, x, g.reshape(1, D_MODEL), b.reshape(1, D_MODEL))


def _staircase_pairs(k):
    return [(a, b) for a in range(k) for b in range(k) if (a + 1) * (b + 1) <= k]


def _kth_largest(vals, k):
    neg = -jnp.inf
    cur = functools.reduce(jnp.maximum, vals)
    for _ in range(k - 1):
        nxt = None
        for v in vals:
            c = jnp.where(v < cur, v, neg)
            nxt = c if nxt is None else jnp.maximum(nxt, c)
        cur = nxt
    return cur


def _peer_route_kernel(h_ref, wq_ref, keys_ref, s0_ref, s1_ref, e0_ref, e1_ref, tau_ref,
                       q_s, top_s, *, tm):
    q_s[...] = jnp.dot(h_ref[...], wq_ref[...], preferred_element_type=F32).astype(BF16)
    neg = -jnp.inf
    for hd in range(PEER_HEADS):
        for c in range(2):
            col = (hd * 2 + c) * PEER_HALF
            s = lax.dot_general(keys_ref[hd, c], q_s[:, col:col + PEER_HALF], NT_DIMS,
                                preferred_element_type=F32)
            (s0_ref if c == 0 else s1_ref)[hd] = s
            cur = jnp.max(s, axis=0, keepdims=True)
            top_s[c, 0, hd:hd + 1, :] = cur
            for a in range(1, PEER_TOPK):
                cur = jnp.max(jnp.where(s < cur, s, neg), axis=0, keepdims=True)
                top_s[c, a, hd:hd + 1, :] = cur
    pairs = _staircase_pairs(PEER_TOPK)
    sums = [top_s[0, a] + top_s[1, b] for a, b in pairs]
    tau = _kth_largest(sums, PEER_TOPK)
    best = top_s[0, 0] + top_s[1, 0]
    z = None
    for v in sums:
        e = jnp.where(v >= tau, jnp.exp(v - best), 0.0)
        z = e if z is None else z + e
    tau_ref[...] = tau
    inv_z = 1.0 / z
    for hd in range(PEER_HEADS):
        e0_ref[hd] = jnp.exp(s0_ref[hd] - top_s[0, 0, hd:hd + 1, :]) * inv_z[hd:hd + 1, :]
        e1_ref[hd] = jnp.exp(s1_ref[hd] - top_s[1, 0, hd:hd + 1, :])


def _peer_route(h_bf, wq_bf, keys_bf, tm):
    m = h_bf.shape[0]
    tab = pl.BlockSpec((PEER_HEADS, PEER_KEYS, tm), lambda i: (0, 0, i))
    tab_shape = jax.ShapeDtypeStruct((PEER_HEADS, PEER_KEYS, m), F32)
    return pl.pallas_call(
        functools.partial(_peer_route_kernel, tm=tm),
        grid=(m // tm,),
        in_specs=[pl.BlockSpec((tm, D_MODEL), lambda i: (i, 0)),
                  pl.BlockSpec((D_MODEL, D_MODEL), lambda i: (0, 0)),
                  pl.BlockSpec((PEER_HEADS, 2, PEER_KEYS, PEER_HALF), lambda i: (0, 0, 0, 0))],
        out_specs=[tab, tab, tab, tab, pl.BlockSpec((PEER_HEADS, tm), lambda i: (0, i))],
        out_shape=[tab_shape, tab_shape, tab_shape, tab_shape,
                   jax.ShapeDtypeStruct((PEER_HEADS, m), F32)],
        scratch_shapes=[pltpu.VMEM((tm, D_MODEL), BF16),
                        pltpu.VMEM((2, PEER_TOPK, PEER_HEADS, tm), F32)],
        compiler_params=_params("arbitrary"),
        name="peer_route",
    )(h_bf, wq_bf, keys_bf)


def _peer_expert_kernel(h_ref, u_ref, vt_ref, s0_ref, s1_ref, e0_ref, e1_ref, tau_ref, o_ref,
                        p_s, *, eb):
    e = pl.program_id(1)
    ht = lax.dot_general(u_ref[...], h_ref[...], NT_DIMS, preferred_element_type=F32)
    for ii in range(eb // PEER_KEYS):
        i = e * (eb // PEER_KEYS) + ii
        w = None
        for hd in range(PEER_HEADS):
            s0 = s0_ref[hd, pl.ds(i, 1), :]
            e0 = e0_ref[hd, pl.ds(i, 1), :]
            tau = tau_ref[hd:hd + 1, :]
            sel = jnp.where(s0 + s1_ref[hd] >= tau, e0 * e1_ref[hd], 0.0)
            w = sel if w is None else w + sel
        rows = slice(ii * PEER_KEYS, (ii + 1) * PEER_KEYS)
        p_s[rows, :] = (_gelu_tanh(ht[rows, :]) * w).astype(BF16)
    contrib = jnp.dot(vt_ref[...], p_s[...], preferred_element_type=F32)

    @pl.when(e == 0)
    def _():
        o_ref[...] = contrib

    @pl.when(e != 0)
    def _():
        o_ref[...] += contrib


def _peer_experts(h_bf, u_bf, vt_bf, route, tb, eb):
    m = h_bf.shape[0]
    n_exp = u_bf.shape[0]
    s0, s1, e0, e1, tau = route
    tab = pl.BlockSpec((PEER_HEADS, PEER_KEYS, tb), lambda t, e: (0, 0, t))
    return pl.pallas_call(
        functools.partial(_peer_expert_kernel, eb=eb),
        grid=(m // tb, n_exp // eb),
        in_specs=[pl.BlockSpec((tb, D_MODEL), lambda t, e: (t, 0)),
                  pl.BlockSpec((eb, D_MODEL), lambda t, e: (e, 0)),
                  pl.BlockSpec((D_MODEL, eb), lambda t, e: (0, e)),
                  tab, tab, tab, tab,
                  pl.BlockSpec((PEER_HEADS, tb), lambda t, e: (0, t))],
        out_specs=pl.BlockSpec((D_MODEL, tb), lambda t, e: (0, t)),
        out_shape=jax.ShapeDtypeStruct((D_MODEL, m), F32),
        scratch_shapes=[pltpu.VMEM((eb, tb), BF16)],
        compiler_params=_params("arbitrary", "arbitrary"),
        name="peer_experts",
    )(h_bf, u_bf, vt_bf, s0, s1, e0, e1, tau)


def _ffn_ln_kernel(ft_ref, h_ref, g_ref, b_ref, y_ref, yb_ref, *, alpha):
    y = _layer_norm(alpha * h_ref[...] + ft_ref[...].T, g_ref[...], b_ref[...])
    y_ref[...] = y
    yb_ref[...] = y.astype(BF16)


def _ffn_ln(ffn_t, h, g, b, alpha, tm):
    m = h.shape[0]
    row = pl.BlockSpec((tm, D_MODEL), lambda i: (i, 0))
    vec = pl.BlockSpec((1, D_MODEL), lambda i: (0, 0))
    return pl.pallas_call(
        functools.partial(_ffn_ln_kernel, alpha=alpha),
        grid=(m // tm,),
        in_specs=[pl.BlockSpec((D_MODEL, tm), lambda i: (0, i)), row, vec, vec],
        out_specs=[row, row],
        out_shape=[jax.ShapeDtypeStruct((m, D_MODEL), F32), jax.ShapeDtypeStruct((m, D_MODEL), BF16)],
        compiler_params=_params("arbitrary"),
        name="ffn_ln",
    )(ffn_t, h, g.reshape(1, D_MODEL), b.reshape(1, D_MODEL))


def _row_tile(m, pref):
    t = min(m, pref)
    assert m % t == 0, (m, t)
    return t


def _token_mix_tail(x, x_bf, a_out, b_out, c_out, w, lp, alpha):
    m = x.shape[0]
    merged = _merge(x_bf, a_out, b_out, c_out, w['w_in'], w['w_branch'],
                    _row_tile(m, 512), 512)
    h, h_bf = _out_ln(merged, w['w_out'], x, lp['ln1_g'], lp['ln1_b'], alpha, _row_tile(m, 256))
    route = _peer_route(h_bf, w['peer_w_q'], w['peer_keys'], _row_tile(m, 256))
    ffn_t = _peer_experts(h_bf, w['peer_u'], w['peer_vt'], route, _row_tile(m, 512), 512)
    return _ffn_ln(ffn_t, h, lp['ln2_g'], lp['ln2_b'], alpha, _row_tile(m, 256))


def _prompt_layer(x, x_bf, lp, w, ssm, rope_tabs, bsz, t, lambda_init, alpha):
    proj = _project(x_bf, w['w_in'], rope_tabs, _row_tile(bsz * t, 512))
    a_out = _sb_prompt(proj, bsz, t, _row_tile(t, 256))
    c_out = _da_prompt(proj, lp, bsz, t, _row_tile(t, 256), lambda_init)
    nseg = 8 // bsz if 8 % bsz == 0 and t % (8 // bsz) == 0 else 1
    assert (bsz * nseg) % 8 == 0
    seg_len = t // nseg
    b_out, h_re, h_im = _s5_prompt(proj[3], ssm, bsz, t, nseg, _row_tile(seg_len, 64))
    y, y_bf = _token_mix_tail(x, x_bf, a_out, b_out, c_out, w, lp, alpha)
    state = (proj[1], proj[2], proj[5], proj[6], h_re, h_im)
    return y, y_bf, state


def _sample_layer(x, x_bf, lp, w, ssm, rope_tabs, caches, layer, page_table, h0_re, h0_im,
                  nb, t, lambda_init, alpha):
    proj = _project(x_bf, w['w_in'], rope_tabs, nb * t)
    proj_s = jnp.pad(proj.reshape(N_PROJ_GROUPS, nb, t, BRANCH_WIDTH),
                     ((0, 0), (0, 0), (0, SAMPLE_ROWS - t), (0, 0)))
    a_pad, c_pad = _sample_attn(proj_s, caches, layer, page_table, lp, t, lambda_init)
    a_out = a_pad[:, :t].reshape(nb * t, BRANCH_WIDTH)
    c_out = c_pad[:, :t].reshape(nb * t, BRANCH_WIDTH)
    b_out, h_re, h_im = _s5_sample(proj[3], ssm, h0_re, h0_im, nb, t)
    y, y_bf = _token_mix_tail(x, x_bf, a_out, b_out, c_out, w, lp, alpha)
    state = (proj[1], proj[2], proj[5], proj[6], h_re, h_im)
    return y, y_bf, state


def kernel(x_prompt, x_sample, cache_sb_k, cache_sb_v, cache_da_k, cache_da_v, state_ssm_re, state_ssm_im, page_table, w_in, w_branch, w_out, ssm_lam_re, ssm_lam_im, ssm_log_dt, ssm_b_re, ssm_b_im, ssm_c_re, ssm_c_im, ssm_d, ssm_w_glu, da_lam_q1, da_lam_k1, da_lam_q2, da_lam_k2, da_norm_g, ln1_g, ln1_b, peer_w_q, peer_keys, peer_u, peer_v, ln2_g, ln2_b):
    depth = w_in.shape[0]
    bsz, t, _ = x_prompt.shape
    nb, ts, _ = x_sample.shape
    n_past = page_table.shape[1] * PAGE_SIZE
    alpha = (2 * depth) ** 0.25
    n_pool = cache_sb_k.shape[1]
    caches = tuple(c.reshape(depth, n_pool, PAGE_SIZE, 512)
                   for c in (cache_sb_k, cache_sb_v, cache_da_k, cache_da_v))
    rope_p = _rope_tables(jnp.tile(jnp.arange(t, dtype=jnp.int32), bsz))
    rope_s = _rope_tables(jnp.tile(n_past + jnp.arange(ts, dtype=jnp.int32), nb))

    yp = x_prompt.reshape(bsz * t, D_MODEL)
    ys = x_sample.reshape(nb * ts, D_MODEL)
    yp_bf = yp.astype(BF16)
    ys_bf = ys.astype(BF16)
    rows_p, rows_s = [], []
    for l in range(depth):
        lp = {'ssm_lam_re': ssm_lam_re[l], 'ssm_lam_im': ssm_lam_im[l], 'ssm_log_dt': ssm_log_dt[l],
              'ssm_b_re': ssm_b_re[l], 'ssm_b_im': ssm_b_im[l],
              'ssm_c_re': ssm_c_re[l], 'ssm_c_im': ssm_c_im[l],
              'ssm_d': ssm_d[l], 'ssm_w_glu': ssm_w_glu[l],
              'da_lam_q1': da_lam_q1[l], 'da_lam_k1': da_lam_k1[l],
              'da_lam_q2': da_lam_q2[l], 'da_lam_k2': da_lam_k2[l], 'da_norm_g': da_norm_g[l],
              'ln1_g': ln1_g[l], 'ln1_b': ln1_b[l], 'ln2_g': ln2_g[l], 'ln2_b': ln2_b[l]}
        w = {'w_in': w_in[l].astype(BF16), 'w_branch': w_branch[l].astype(BF16),
             'w_out': w_out[l].astype(BF16), 'peer_w_q': peer_w_q[l].astype(BF16),
             'peer_keys': peer_keys[l].astype(BF16), 'peer_u': peer_u[l].astype(BF16),
             'peer_vt': peer_v[l].astype(BF16).T}
        ssm = _ssm_params(lp)
        lambda_init = 0.8 - 0.6 * math.exp(-0.3 * l)
        yp, yp_bf, st_p = _prompt_layer(yp, yp_bf, lp, w, ssm, rope_p, bsz, t, lambda_init, alpha)
        ys, ys_bf, st_s = _sample_layer(ys, ys_bf, lp, w, ssm, rope_s, caches, l, page_table,
                                        state_ssm_re[l], state_ssm_im[l], nb, ts, lambda_init, alpha)
        rows_p.append(st_p)
        rows_s.append(st_s)

    def pack(rows, lead):
        sk, sv, dk, dv, hr, hi = [jnp.stack(z) for z in zip(*rows)]
        n = lead[0]
        return (sk.reshape(depth, *lead, SB_HEADS, SB_HEAD_DIM),
                sv.reshape(depth, *lead, SB_HEADS, SB_HEAD_DIM),
                dk.reshape(depth, *lead, DA_HEADS, 2, DA_SUB_DIM),
                dv.reshape(depth, *lead, DA_HEADS, DA_V_DIM),
                hr.reshape(depth, n, SSM_GROUPS, SSM_STATE),
                hi.reshape(depth, n, SSM_GROUPS, SSM_STATE))

    return ((yp.reshape(bsz, t, D_MODEL), ys.reshape(nb, ts, D_MODEL))
            + pack(rows_p, (bsz, t)) + pack(rows_s, (nb, ts)))
```

```python
import functools
import math

import jax
import jax.numpy as jnp
from jax import lax
from jax.experimental import pallas as pl
from jax.experimental.pallas import tpu as pltpu

F32 = jnp.float32
BF16 = jnp.bfloat16

D_MODEL = 2048
PAGE_SIZE = 128
SB_HEADS = 4
SB_HEAD_DIM = 128
SSM_GROUP = 16
SSM_GROUPS = 32
SSM_STATE = 64
SSM_WIDTH = SSM_GROUPS * SSM_GROUP
SSM_LANES = SSM_GROUPS * SSM_STATE
DA_HEADS = 4
DA_SUB_DIM = 64
DA_V_DIM = 128
ROPE_THETA = 500000.0
ROPE_DIM = DA_SUB_DIM // 4
N_BRANCH = 3
BRANCH_WIDTH = 512
N_PROJ_GROUPS = 7
GATE_COL0 = N_PROJ_GROUPS * BRANCH_WIDTH
PEER_HEADS = 8
PEER_KEYS = 128
PEER_HALF = 128
PEER_TOPK = 16
LN_EPS = 1e-5

V7X_VMEM_BYTES = 64 * 1024 * 1024
VMEM_LIMIT = V7X_VMEM_BYTES - 8 * 1024 * 1024

NT_DIMS = (((1,), (1,)), ((), ()))

F32_EXP_UNDERFLOW = -105.0


def _params(*sem):
    return pltpu.CompilerParams(dimension_semantics=sem, vmem_limit_bytes=VMEM_LIMIT)


def _gelu_tanh(x):
    return 0.5 * x * (1.0 + jnp.tanh(math.sqrt(2.0 / math.pi) * (x + 0.044715 * (x * x * x))))


def _sigmoid(x):
    return 1.0 / (1.0 + jnp.exp(-x))


def _proj_kernel(x_ref, w_ref, cos_ref, sa_ref, sb_ref, o_ref):
    j = pl.program_id(0)
    o_ref[0] = jnp.dot(x_ref[...], w_ref[...], preferred_element_type=F32)

    @pl.when((j == 4) | (j == 5))
    def _():
        c = cos_ref[...]
        sa = sa_ref[...]
        sb = sb_ref[...]
        for t in range(BRANCH_WIDTH // 128):
            xt = o_ref[0, :, t * 128:(t + 1) * 128]
            o_ref[0, :, t * 128:(t + 1) * 128] = (
                xt * c + pltpu.roll(xt, 8, 1) * sa + pltpu.roll(xt, 120, 1) * sb)


def _project(x_bf, w_in_bf, rope_tabs, tm):
    m = x_bf.shape[0]
    nr = m // tm
    cos_t, sa_t, sb_t = rope_tabs

    def tab_map(j, i):
        return (jnp.where((j == 4) | (j == 5), i, 0), 0)

    return pl.pallas_call(
        _proj_kernel,
        grid=(N_PROJ_GROUPS, nr),
        in_specs=[
            pl.BlockSpec((tm, D_MODEL), lambda j, i: (i, 0)),
            pl.BlockSpec((D_MODEL, BRANCH_WIDTH), lambda j, i: (0, j)),
            pl.BlockSpec((tm, 128), tab_map),
            pl.BlockSpec((tm, 128), tab_map),
            pl.BlockSpec((tm, 128), tab_map),
        ],
        out_specs=pl.BlockSpec((1, tm, BRANCH_WIDTH), lambda j, i: (j, i, 0)),
        out_shape=jax.ShapeDtypeStruct((N_PROJ_GROUPS, m, BRANCH_WIDTH), F32),
        compiler_params=_params("arbitrary", "arbitrary"),
        name="proj",
    )(x_bf, w_in_bf, cos_t, sa_t, sb_t)


def _rope_tables(pos):
    half = ROPE_DIM // 2
    inv_freq = ROPE_THETA ** (-jnp.arange(half, dtype=F32) / half)
    ang = pos.astype(F32)[:, None] * inv_freq[None, :]
    cos = jnp.cos(ang)
    sin = jnp.sin(ang)
    n = pos.shape[0]
    ones = jnp.ones((n, DA_SUB_DIM - ROPE_DIM), F32)
    zeros = jnp.zeros((n, DA_SUB_DIM - ROPE_DIM), F32)
    zh = jnp.zeros((n, half), F32)
    c64 = jnp.concatenate([cos, cos, ones], axis=1)
    sa64 = jnp.concatenate([zh, sin, zeros], axis=1)
    sb64 = jnp.concatenate([-sin, zh, zeros], axis=1)
    return tuple(jnp.concatenate([t, t], axis=1) for t in (c64, sa64, sb64))


def _log_sigmoid(z):
    return jnp.minimum(z, 0.0) - jnp.log1p(jnp.exp(-jnp.abs(z)))


def _split_bf16(x):
    hi = x.astype(BF16)
    lo = (x - hi.astype(F32)).astype(BF16)
    return hi, lo


def _sb_block(q, kb, vb, tri, c, acc, scale, mask):
    tk = kb.shape[0]
    z = lax.dot_general(q, kb, NT_DIMS, preferred_element_type=F32) * scale
    lb = _log_sigmoid(z)
    lk = lb - z
    if mask is not None:
        lk = jnp.where(mask, lk, 0.0)
    hi, lo = _split_bf16(lk)
    r = (jnp.dot(hi, tri, preferred_element_type=F32)
         + jnp.dot(lo, tri, preferred_element_type=F32))
    w = jnp.exp(lb + r[:, :tk] + c)
    if mask is not None:
        w = jnp.where(mask, w, 0.0)
    acc = acc + jnp.dot(w.astype(BF16), vb, preferred_element_type=F32)
    return c + r[:, tk:], acc


def _sb_prompt_kernel(q_ref, k_ref, v_ref, tri_ref, o_ref, kbf, vbf, *, blk, scale):
    qi = pl.program_id(2)

    @pl.when(qi == 0)
    def _():
        kbf[...] = k_ref[0].astype(BF16)
        vbf[...] = v_ref[0].astype(BF16)

    q = q_ref[0].astype(BF16)
    tri = tri_ref[...]
    row = lax.broadcasted_iota(jnp.int32, (blk, blk), 0)
    col = lax.broadcasted_iota(jnp.int32, (blk, blk), 1)
    off = pl.multiple_of(qi * blk, blk)
    c0 = jnp.zeros((blk, blk), F32)
    a0 = jnp.zeros((blk, SB_HEAD_DIM), F32)
    c, acc = _sb_block(q, kbf[pl.ds(off, blk), :], vbf[pl.ds(off, blk), :], tri, c0, a0,
                       scale, col < row)

    def cond(carry):
        jj, _, _, cmax = carry
        return (jj < qi) & (cmax > F32_EXP_UNDERFLOW)

    def body(carry):
        jj, c, acc, _ = carry
        o = pl.multiple_of((qi - 1 - jj) * blk, blk)
        c, acc = _sb_block(q, kbf[pl.ds(o, blk), :], vbf[pl.ds(o, blk), :], tri, c, acc, scale, None)
        return jj + 1, c, acc, jnp.max(c[:, :128])

    _, _, acc, _ = lax.while_loop(cond, body, (jnp.int32(0), c, acc, jnp.max(c[:, :128])))
    o_ref[...] = acc


def _tri_ones(blk):
    j = jnp.arange(blk)[:, None]
    s = jnp.arange(blk)[None, :]
    return jnp.concatenate([(j > s).astype(BF16), jnp.ones((blk, blk), BF16)], axis=1)


def _sb_prompt(proj, bsz, t, blk):
    nq = t // blk
    kern = functools.partial(_sb_prompt_kernel, blk=blk, scale=SB_HEAD_DIM ** -0.5)
    return pl.pallas_call(
        kern,
        grid=(bsz, SB_HEADS, nq),
        in_specs=[
            pl.BlockSpec((1, blk, SB_HEAD_DIM), lambda b, h, i: (0, b * nq + i, h)),
            pl.BlockSpec((1, t, SB_HEAD_DIM), lambda b, h, i: (1, b, h)),
            pl.BlockSpec((1, t, SB_HEAD_DIM), lambda b, h, i: (2, b, h)),
            pl.BlockSpec((blk, 2 * blk), lambda b, h, i: (0, 0)),
        ],
        out_specs=pl.BlockSpec((blk, SB_HEAD_DIM), lambda b, h, i: (b * nq + i, h)),
        out_shape=jax.ShapeDtypeStruct((bsz * t, SB_HEADS * SB_HEAD_DIM), F32),
        scratch_shapes=[pltpu.VMEM((t, SB_HEAD_DIM), BF16), pltpu.VMEM((t, SB_HEAD_DIM), BF16)],
        compiler_params=_params("arbitrary", "arbitrary", "arbitrary"),
        name="sb_prompt",
    )(proj, proj, proj, _tri_ones(blk))


def _da_lambda(lq1, lk1, lq2, lk2, h, lambda_init):
    a = jnp.sum(lq1[pl.ds(h, 1), :] * lk1[pl.ds(h, 1), :], axis=1, keepdims=True)
    b = jnp.sum(lq2[pl.ds(h, 1), :] * lk2[pl.ds(h, 1), :], axis=1, keepdims=True)
    return jnp.exp(a) - jnp.exp(b) + lambda_init


def _split_components(q):
    lane = lax.broadcasted_iota(jnp.int32, q.shape, 1)
    q0 = jnp.where(lane < DA_SUB_DIM, q, 0.0).astype(BF16)
    q1 = jnp.where(lane >= DA_SUB_DIM, q, 0.0).astype(BF16)
    return q0, q1


def _softmax_block(qc, kb, vb, m, l, acc, scale, mask):
    s = lax.dot_general(qc, kb, NT_DIMS, preferred_element_type=F32) * scale
    if mask is not None:
        s = jnp.where(mask, s, -jnp.inf)
    m_new = jnp.maximum(m, jnp.max(s, axis=1, keepdims=True))
    p = jnp.exp(s - m_new)
    alpha = jnp.exp(m - m_new)
    l = alpha * l + jnp.sum(p, axis=1, keepdims=True)
    acc = alpha * acc + jnp.dot(p.astype(BF16), vb, preferred_element_type=F32)
    return m_new, l, acc


def _da_finish(st0, st1, lam, g, lambda_init):
    o = st0[2] / st0[1] - lam * (st1[2] / st1[1])
    o = o * lax.rsqrt(jnp.mean(o * o, axis=-1, keepdims=True) + LN_EPS) * g
    return (1.0 - lambda_init) * o


def _da_prompt_kernel(q_ref, k_ref, v_ref, lq1, lk1, lq2, lk2, g_ref, o_ref, kbf, vbf,
                      *, blk, scale, lambda_init):
    h = pl.program_id(1)
    qi = pl.program_id(2)

    @pl.when(qi == 0)
    def _():
        kbf[...] = k_ref[0].astype(BF16)
        vbf[...] = v_ref[0].astype(BF16)

    q0, q1 = _split_components(q_ref[0])
    row = lax.broadcasted_iota(jnp.int32, (blk, blk), 0)
    col = lax.broadcasted_iota(jnp.int32, (blk, blk), 1)
    mask = col <= row
    off = pl.multiple_of(qi * blk, blk)
    init = (jnp.full((blk, 1), -jnp.inf, F32), jnp.zeros((blk, 1), F32),
            jnp.zeros((blk, DA_V_DIM), F32))
    kd = kbf[pl.ds(off, blk), :]
    vd = vbf[pl.ds(off, blk), :]
    st0 = _softmax_block(q0, kd, vd, *init, scale, mask)
    st1 = _softmax_block(q1, kd, vd, *init, scale, mask)

    def body(j, carry):
        o = pl.multiple_of(j * blk, blk)
        kb = kbf[pl.ds(o, blk), :]
        vb = vbf[pl.ds(o, blk), :]
        return (_softmax_block(q0, kb, vb, *carry[0], scale, None),
                _softmax_block(q1, kb, vb, *carry[1], scale, None))

    st0, st1 = lax.fori_loop(0, qi, body, (st0, st1))
    lam = _da_lambda(lq1, lk1, lq2, lk2, h, lambda_init)
    o_ref[...] = _da_finish(st0, st1, lam, g_ref[...], lambda_init)


def _da_prompt(proj, lp, bsz, t, blk, lambda_init):
    nq = t // blk
    kern = functools.partial(_da_prompt_kernel, blk=blk, scale=DA_SUB_DIM ** -0.5,
                             lambda_init=lambda_init)
    lam_spec = pl.BlockSpec((DA_HEADS, DA_SUB_DIM), lambda b, h, i: (0, 0))
    return pl.pallas_call(
        kern,
        grid=(bsz, DA_HEADS, nq),
        in_specs=[
            pl.BlockSpec((1, blk, 128), lambda b, h, i: (4, b * nq + i, h)),
            pl.BlockSpec((1, t, 128), lambda b, h, i: (5, b, h)),
            pl.BlockSpec((1, t, DA_V_DIM), lambda b, h, i: (6, b, h)),
            lam_spec, lam_spec, lam_spec, lam_spec,
            pl.BlockSpec((1, DA_V_DIM), lambda b, h, i: (0, 0)),
        ],
        out_specs=pl.BlockSpec((blk, DA_V_DIM), lambda b, h, i: (b * nq + i, h)),
        out_shape=jax.ShapeDtypeStruct((bsz * t, DA_HEADS * DA_V_DIM), F32),
        scratch_shapes=[pltpu.VMEM((t, 128), BF16), pltpu.VMEM((t, DA_V_DIM), BF16)],
        compiler_params=_params("arbitrary", "arbitrary", "arbitrary"),
        name="da_prompt",
    )(proj, proj, proj, lp['da_lam_q1'], lp['da_lam_k1'], lp['da_lam_q2'], lp['da_lam_k2'],
      lp['da_norm_g'].reshape(1, DA_V_DIM))


SAMPLE_ROWS = 8
PAGES_PER_STEP = 4
PAGE_ROWS = PAGE_SIZE * SB_HEADS


def _tri_t_aug():
    s = jnp.arange(PAGE_SIZE + 8)[:, None]
    j = jnp.arange(PAGE_SIZE)[None, :]
    return ((j > s) | (s >= PAGE_SIZE)).astype(BF16)


def _head_rows(ref, h, lead):
    idx = (pl.ds(h, PAGE_SIZE, stride=SB_HEADS), slice(None))
    return ref[(0,) + idx] if lead else ref[idx]


def _sb_page(qt_ref, k_ref, v_ref, lead, tri_t, c, acc, mask):
    z = None
    for h in range(SB_HEADS):
        kh = _head_rows(k_ref, h, lead).astype(BF16)
        d = jnp.dot(kh, qt_ref[0, h * 128:(h + 1) * 128, :], preferred_element_type=F32)
        z = d if z is None else z + d
    z = z * (SB_HEAD_DIM ** -0.5)
    lb = _log_sigmoid(z)
    lk = lb - z
    if mask is not None:
        lk = jnp.where(mask, lk, 0.0)
    hi, lo = _split_bf16(lk)
    r = (jnp.dot(tri_t, hi, preferred_element_type=F32)
         + jnp.dot(tri_t, lo, preferred_element_type=F32))
    w = jnp.exp(lb + r[:PAGE_SIZE] + c[0:1, :])
    if mask is not None:
        w = jnp.where(mask, w, 0.0)
    wt = w.T
    parts = []
    for h in range(SB_HEADS):
        rows = slice(h * SAMPLE_ROWS, (h + 1) * SAMPLE_ROWS)
        vh = _head_rows(v_ref, h, lead).astype(BF16)
        parts.append(acc[rows] + jnp.dot(wt[rows].astype(BF16), vh, preferred_element_type=F32))
    return c + r[PAGE_SIZE:], jnp.concatenate(parts, axis=0)


def _da_page(qs_ref, kt_ref, v_ref, lead, m, l, acc, mask):
    scale = DA_SUB_DIM ** -0.5
    nr = 2 * SAMPLE_ROWS
    ms, ls, accs = [], [], []
    for h in range(DA_HEADS):
        rows = slice(h * nr, (h + 1) * nr)
        kt = (kt_ref[0, h * 128:(h + 1) * 128, :] if lead else kt_ref[h * 128:(h + 1) * 128, :])
        s = jnp.dot(qs_ref[0, h], kt.astype(BF16), preferred_element_type=F32) * scale
        if mask is not None:
            s = jnp.where(mask, s, -jnp.inf)
        m_old = m[rows]
        m_new = jnp.maximum(m_old, jnp.max(s, axis=1, keepdims=True))
        p = jnp.exp(s - m_new)
        alpha = jnp.exp(m_old - m_new)
        vh = _head_rows(v_ref, h, lead).astype(BF16)
        ms.append(m_new)
        ls.append(alpha * l[rows] + jnp.sum(p, axis=1, keepdims=True))
        accs.append(alpha * acc[rows] + jnp.dot(p.astype(BF16), vh, preferred_element_type=F32))
    return (jnp.concatenate(ms, axis=0), jnp.concatenate(ls, axis=0), jnp.concatenate(accs, axis=0))


def _sample_attn_kernel(pt_ref, qt_ref, qs_ref, nk_ref, nv_ref, dnkt_ref, dnv_ref, *rest,
                        n_new, lambda_init):
    del pt_ref
    npp = PAGES_PER_STEP
    cache_refs = rest[:4 * npp]
    tri_ref, lq1, lk1, lq2, lk2, g_ref, oa_ref, oc_ref, c_s, acc_s, m_s, l_s, dacc_s = rest[4 * npp:]
    s = pl.program_id(1)
    tri_t = tri_ref[...]
    nr = 2 * SAMPLE_ROWS

    @pl.when(s == 0)
    def _():
        key = lax.broadcasted_iota(jnp.int32, (PAGE_SIZE, 128), 0)
        qry = lax.broadcasted_iota(jnp.int32, (PAGE_SIZE, 128), 1) % SAMPLE_ROWS
        c, acc = _sb_page(qt_ref, nk_ref, nv_ref, True, tri_t,
                          jnp.zeros((SAMPLE_ROWS, 128), F32),
                          jnp.zeros((SB_HEADS * SAMPLE_ROWS, 128), F32),
                          (key < n_new) & (key < qry))
        c_s[...] = c
        acc_s[...] = acc
        qry = lax.broadcasted_iota(jnp.int32, (nr, PAGE_SIZE), 0) % SAMPLE_ROWS
        key = lax.broadcasted_iota(jnp.int32, (nr, PAGE_SIZE), 1)
        m, l, dacc = _da_page(qs_ref, dnkt_ref, dnv_ref, True,
                              jnp.full((DA_HEADS * nr, 128), -jnp.inf, F32),
                              jnp.zeros((DA_HEADS * nr, 128), F32),
                              jnp.zeros((DA_HEADS * nr, 128), F32),
                              (key < n_new) & (key <= qry))
        m_s[...] = m
        l_s[...] = l
        dacc_s[...] = dacc

    c, acc = c_s[...], acc_s[...]
    m, l, dacc = m_s[...], l_s[...], dacc_s[...]
    for k in range(npp):
        ck, cv, dck, dcv = cache_refs[4 * k:4 * k + 4]
        c, acc = _sb_page(qt_ref, ck, cv, False, tri_t, c, acc, None)
        m, l, dacc = _da_page(qs_ref, dck, dcv, False, m, l, dacc, None)
    c_s[...] = c
    acc_s[...] = acc
    m_s[...] = m
    l_s[...] = l
    dacc_s[...] = dacc

    @pl.when(s == pl.num_programs(1) - 1)
    def _():
        oa_ref[0] = acc
        for h in range(DA_HEADS):
            lam = _da_lambda(lq1, lk1, lq2, lk2, h, lambda_init)
            r0 = slice(h * nr, h * nr + SAMPLE_ROWS)
            r1 = slice(h * nr + SAMPLE_ROWS, (h + 1) * nr)
            oc_ref[0, h] = _da_finish((None, l[r0], dacc[r0]), (None, l[r1], dacc[r1]),
                                      lam, g_ref[...], lambda_init)


def _sample_attn(proj, caches, layer, page_table, lp, nb, ts, lambda_init):
    n_pages = page_table.shape[1]
    npp = PAGES_PER_STEP
    assert n_pages % npp == 0 and ts <= SAMPLE_ROWS
    pad_q = ((0, 0), (0, SAMPLE_ROWS - ts), (0, 0), (0, 0))

    q = jnp.pad(proj[0].reshape(nb, ts, SB_HEADS, SB_HEAD_DIM), pad_q)
    qt = jnp.einsum('bihd,hg->bhdgi', q, jnp.eye(SB_HEADS, dtype=F32))
    qt = qt.reshape(nb, SB_HEADS * SB_HEAD_DIM, SB_HEADS * SAMPLE_ROWS)
    qt = jnp.pad(qt, ((0, 0), (0, 0), (0, 128 - SB_HEADS * SAMPLE_ROWS))).astype(BF16)
    dq = jnp.pad(proj[4].reshape(nb, ts, DA_HEADS, 128), pad_q).transpose(0, 2, 1, 3)
    comp = (jnp.arange(128) // DA_SUB_DIM)[None, :] == jnp.arange(2)[:, None]
    qs = jnp.where(comp[None, None, :, None, :], dq[:, :, None], 0.0)
    qs = qs.reshape(nb, DA_HEADS, 2 * SAMPLE_ROWS, 128).astype(BF16)

    def new_page(g):
        return jnp.pad(proj[g].reshape(nb, ts * SB_HEADS, 128),
                       ((0, 0), (0, PAGE_ROWS - ts * SB_HEADS), (0, 0)))

    dnkt = jnp.pad(proj[5].reshape(nb, ts, 512).transpose(0, 2, 1),
                   ((0, 0), (0, 0), (0, PAGE_SIZE - ts)))

    kern = functools.partial(_sample_attn_kernel, n_new=ts, lambda_init=lambda_init)
    per_seq = lambda b, s, pt: (b, 0, 0)
    page_spec = pl.BlockSpec((1, PAGE_ROWS, 128), per_seq)

    def cache_spec(k):
        return pl.BlockSpec((None, None, PAGE_ROWS, 128),
                            lambda b, s, pt: (layer, pt[b, n_pages - 1 - (s * npp + k)], 0, 0))

    const2 = lambda b, s, pt: (0, 0)
    lam_spec = pl.BlockSpec((DA_HEADS, DA_SUB_DIM), const2)
    cache_specs, cache_args = [], []
    for k in range(npp):
        cache_specs += [cache_spec(k)] * 4
        cache_args += list(caches)
    n_state = DA_HEADS * 2 * SAMPLE_ROWS
    grid_spec = pltpu.PrefetchScalarGridSpec(
        num_scalar_prefetch=1,
        grid=(nb, n_pages // npp),
        in_specs=[pl.BlockSpec((1, SB_HEADS * SB_HEAD_DIM, 128), per_seq),
                  pl.BlockSpec((1, DA_HEADS, 2 * SAMPLE_ROWS, 128), lambda b, s, pt: (b, 0, 0, 0)),
                  page_spec, page_spec, page_spec, page_spec]
                 + cache_specs
                 + [pl.BlockSpec((PAGE_SIZE + 8, PAGE_SIZE), const2),
                    lam_spec, lam_spec, lam_spec, lam_spec,
                    pl.BlockSpec((1, DA_V_DIM), const2)],
        out_specs=[pl.BlockSpec((1, SB_HEADS * SAMPLE_ROWS, 128), per_seq),
                   pl.BlockSpec((1, DA_HEADS, SAMPLE_ROWS, 128), lambda b, s, pt: (b, 0, 0, 0))],
        scratch_shapes=[pltpu.VMEM((SAMPLE_ROWS, 128), F32),
                        pltpu.VMEM((SB_HEADS * SAMPLE_ROWS, 128), F32),
                        pltpu.VMEM((n_state, 128), F32), pltpu.VMEM((n_state, 128), F32),
                        pltpu.VMEM((n_state, 128), F32)],
    )
    oa, oc = pl.pallas_call(
        kern, grid_spec=grid_spec,
        out_shape=[jax.ShapeDtypeStruct((nb, SB_HEADS * SAMPLE_ROWS, 128), F32),
                   jax.ShapeDtypeStruct((nb, DA_HEADS, SAMPLE_ROWS, 128), F32)],
        compiler_params=_params("arbitrary", "arbitrary"),
        name="sample_attn",
    )(page_table, qt, qs, new_page(1), new_page(2), dnkt, new_page(6), *cache_args,
      _tri_t_aug(), lp['da_lam_q1'], lp['da_lam_k1'], lp['da_lam_q2'], lp['da_lam_k2'],
      lp['da_norm_g'].reshape(1, DA_V_DIM))
    a_out = oa.reshape(nb, SB_HEADS, SAMPLE_ROWS, 128)[:, :, :ts].transpose(0, 2, 1, 3)
    c_out = oc[:, :, :ts].transpose(0, 2, 1, 3)
    return a_out.reshape(nb * ts, BRANCH_WIDTH), c_out.reshape(nb * ts, BRANCH_WIDTH)


LANE_GROUP = 512


def _s5_kernel(u_ref, bb_ref, a_ref, h0_ref, cc_ref, d_ref, glu_ref, y_ref, hf_ref,
               bu_s, st_s, *, nseq, steps, emit):
    i = pl.program_id(0)
    n = SSM_LANES

    @pl.when(i == 0)
    def _():
        st_s[...] = h0_ref[...]

    u = u_ref[...]
    bu_s[...] = jnp.dot(u.astype(BF16), bb_ref[...], preferred_element_type=F32)

    for sg in range(nseq // 8):
        for lg in range(n // LANE_GROUP):
            lre = slice(lg * LANE_GROUP, (lg + 1) * LANE_GROUP)
            lim = slice(n + lg * LANE_GROUP, n + (lg + 1) * LANE_GROUP)
            rs = slice(sg * 8, (sg + 1) * 8)
            ar = jnp.broadcast_to(a_ref[0:1, lre], (8, LANE_GROUP))
            ai = jnp.broadcast_to(a_ref[0:1, lim], (8, LANE_GROUP))

            def step(t, carry, lre=lre, lim=lim, sg=sg, ar=ar, ai=ai):
                hr, hi = carry
                r0 = pl.multiple_of(t * nseq + sg * 8, 8)
                nr = ar * hr - ai * hi + bu_s[pl.ds(r0, 8), lre]
                ni = ar * hi + ai * hr + bu_s[pl.ds(r0, 8), lim]
                if emit:
                    bu_s[pl.ds(r0, 8), lre] = nr
                    bu_s[pl.ds(r0, 8), lim] = ni
                return nr, ni

            hr, hi = lax.fori_loop(0, steps, step, (st_s[rs, lre], st_s[rs, lim]))
            st_s[rs, lre] = hr
            st_s[rs, lim] = hi

    if emit:
        y = jnp.dot(bu_s[...].astype(BF16), cc_ref[...], preferred_element_type=F32) + d_ref[...] * u
        y = _gelu_tanh(y)
        gate = jnp.dot(y.astype(BF16), glu_ref[...], preferred_element_type=F32)
        y_ref[...] = y * _sigmoid(gate)
    else:
        y_ref[...] = jnp.zeros(y_ref.shape, F32)

    hf_ref[...] = st_s[...]


def _s5_scan(u_rows, ssm, h0, nseq, steps, emit):
    rows = u_rows.shape[0]
    blk_rows = steps * nseq
    n_chunks = rows // blk_rows
    kern = functools.partial(_s5_kernel, nseq=nseq, steps=steps, emit=emit)
    const = lambda i: (0, 0)
    y_rows = blk_rows if emit else 8
    y, hf = pl.pallas_call(
        kern,
        grid=(n_chunks,),
        in_specs=[
            pl.BlockSpec((blk_rows, SSM_WIDTH), lambda i: (i, 0)),
            pl.BlockSpec((SSM_WIDTH, 2 * SSM_LANES), const),
            pl.BlockSpec((1, 2 * SSM_LANES), const),
            pl.BlockSpec((nseq, 2 * SSM_LANES), const),
            pl.BlockSpec((2 * SSM_LANES, SSM_WIDTH), const),
            pl.BlockSpec((1, SSM_WIDTH), const),
            pl.BlockSpec((SSM_WIDTH, SSM_WIDTH), const),
        ],
        out_specs=[pl.BlockSpec((y_rows, SSM_WIDTH), (lambda i: (i, 0)) if emit else const),
                   pl.BlockSpec((nseq, 2 * SSM_LANES), const)],
        out_shape=[jax.ShapeDtypeStruct((rows if emit else 8, SSM_WIDTH), F32),
                   jax.ShapeDtypeStruct((nseq, 2 * SSM_LANES), F32)],
        scratch_shapes=[pltpu.VMEM((blk_rows, 2 * SSM_LANES), F32),
                        pltpu.VMEM((nseq, 2 * SSM_LANES), F32)],
        compiler_params=_params("arbitrary"),
        name="s5_emit" if emit else "s5_state",
    )(u_rows, ssm['bb'], ssm['a'], h0, ssm['cc'], ssm['d'], ssm['glu'])
    return y, hf


def _segment_init_kernel(a_ref, f_ref, o_ref, *, seg_len, nb, nseg):
    n = SSM_LANES
    pr = a_ref[:, :n]
    pi = a_ref[:, n:]
    rr, ri = None, None
    e = seg_len
    while e:
        if e & 1:
            if rr is None:
                rr, ri = pr, pi
            else:
                rr, ri = rr * pr - ri * pi, rr * pi + ri * pr
        e >>= 1
        if e:
            pr, pi = pr * pr - pi * pi, 2.0 * pr * pi
    for b in range(nb):
        hr = jnp.zeros((1, n), F32)
        hi = jnp.zeros((1, n), F32)
        for k in range(nseg):
            r = b * nseg + k
            o_ref[r:r + 1, :n] = hr
            o_ref[r:r + 1, n:] = hi
            fr = f_ref[r:r + 1, :n]
            fi = f_ref[r:r + 1, n:]
            hr, hi = rr * hr - ri * hi + fr, rr * hi + ri * hr + fi


def _segment_init(a, f, seg_len, nb, nseg):
    kern = functools.partial(_segment_init_kernel, seg_len=seg_len, nb=nb, nseg=nseg)
    return pl.pallas_call(
        kern, out_shape=jax.ShapeDtypeStruct(f.shape, F32), name="s5_segments",
        compiler_params=pltpu.CompilerParams(vmem_limit_bytes=VMEM_LIMIT),
    )(a, f)


def _ssm_params(lp):
    lr = lp['ssm_lam_re']
    li = lp['ssm_lam_im']
    dt = jnp.exp(lp['ssm_log_dt'])[:, None]
    mag = jnp.exp(lr * dt)
    ab_re = mag * jnp.cos(li * dt)
    ab_im = mag * jnp.sin(li * dt)
    den = lr * lr + li * li
    f_re = ((ab_re - 1.0) * lr + ab_im * li) / den
    f_im = (ab_im * lr - (ab_re - 1.0) * li) / den
    b_re = lp['ssm_b_re']
    b_im = lp['ssm_b_im']
    bb_re = f_re[..., None] * b_re - f_im[..., None] * b_im
    bb_im = f_re[..., None] * b_im + f_im[..., None] * b_re
    eye = jnp.eye(SSM_GROUPS, dtype=F32)

    def in_blockdiag(w):
        return jnp.einsum('gnp,gh->gphn', w, eye).reshape(SSM_WIDTH, SSM_LANES)

    def out_blockdiag(w):
        return jnp.einsum('gpn,gh->gnhp', w, eye).reshape(SSM_LANES, SSM_WIDTH)

    return {
        'bb': jnp.concatenate([in_blockdiag(bb_re), in_blockdiag(bb_im)], axis=1).astype(BF16),
        'cc': jnp.concatenate([out_blockdiag(lp['ssm_c_re']), -out_blockdiag(lp['ssm_c_im'])],
                              axis=0).astype(BF16),
        'a': jnp.concatenate([ab_re.reshape(1, SSM_LANES), ab_im.reshape(1, SSM_LANES)], axis=1),
        'd': lp['ssm_d'].reshape(1, SSM_WIDTH),
        'glu': lp['ssm_w_glu'].astype(BF16),
    }


def _s5_prompt(u, ssm, bsz, t, nseg, steps):
    nseq = bsz * nseg
    seg_len = t // nseg
    u_rows = u.reshape(bsz, nseg, seg_len, SSM_WIDTH).transpose(2, 0, 1, 3).reshape(seg_len * nseq, SSM_WIDTH)
    zero = jnp.zeros((nseq, 2 * SSM_LANES), F32)
    _, f = _s5_scan(u_rows, ssm, zero, nseq, steps, emit=False)
    h0 = _segment_init(ssm['a'], f, seg_len, bsz, nseg)
    y_rows, hf = _s5_scan(u_rows, ssm, h0, nseq, steps, emit=True)
    y = y_rows.reshape(seg_len, bsz, nseg, SSM_WIDTH).transpose(1, 2, 0, 3).reshape(bsz * t, SSM_WIDTH)
    last = hf.reshape(bsz, nseg, 2 * SSM_LANES)[:, -1]
    return y, last[:, :SSM_LANES], last[:, SSM_LANES:]


def _s5_sample(u, ssm, h0_re, h0_im, nb, t):
    u_rows = u.reshape(nb, t, SSM_WIDTH).transpose(1, 0, 2).reshape(t * nb, SSM_WIDTH)
    h0 = jnp.concatenate([h0_re.reshape(nb, SSM_LANES), h0_im.reshape(nb, SSM_LANES)], axis=1)
    y_rows, hf = _s5_scan(u_rows, ssm, h0, nb, t, emit=True)
    y = y_rows.reshape(t, nb, SSM_WIDTH).transpose(1, 0, 2).reshape(nb * t, SSM_WIDTH)
    return y, hf[:, :SSM_LANES], hf[:, SSM_LANES:]


def _merge_kernel(x_ref, a_ref, b_ref, c_ref, wg0, wg1, wg2, wb_ref, o_ref):
    x = x_ref[...]
    acc = None
    for n, (br, wg) in enumerate(((a_ref, wg0), (b_ref, wg1), (c_ref, wg2))):
        gate = _sigmoid(jnp.dot(x, wg[...], preferred_element_type=F32))
        pb = jnp.dot(br[...].astype(BF16), wb_ref[n], preferred_element_type=F32)
        acc = gate * pb if acc is None else acc + gate * pb
    o_ref[...] = acc.astype(o_ref.dtype)


def _merge(x_bf, a_out, b_out, c_out, w_in_bf, w_branch_bf, tm, tn):
    m = x_bf.shape[0]
    ncol = D_MODEL // tn
    g0 = GATE_COL0 // tn

    def gate_spec(n):
        return pl.BlockSpec((D_MODEL, tn), lambda i, j: (0, g0 + n * ncol + j))

    br_spec = pl.BlockSpec((tm, BRANCH_WIDTH), lambda i, j: (i, 0))
    return pl.pallas_call(
        _merge_kernel,
        grid=(m // tm, ncol),
        in_specs=[pl.BlockSpec((tm, D_MODEL), lambda i, j: (i, 0)), br_spec, br_spec, br_spec,
                  gate_spec(0), gate_spec(1), gate_spec(2),
                  pl.BlockSpec((N_BRANCH, BRANCH_WIDTH, tn), lambda i, j: (0, 0, j))],
        out_specs=pl.BlockSpec((tm, tn), lambda i, j: (i, j)),
        out_shape=jax.ShapeDtypeStruct((m, D_MODEL), BF16),
        compiler_params=_params("arbitrary", "arbitrary"),
        name="merge",
    )(x_bf, a_out, b_out, c_out, w_in_bf, w_in_bf, w_in_bf, w_branch_bf)


def _layer_norm(v, g, b):
    mu = jnp.mean(v, axis=-1, keepdims=True)
    d = v - mu
    var = jnp.mean(d * d, axis=-1, keepdims=True)
    return d * lax.rsqrt(var + LN_EPS) * g + b


def _out_ln_kernel(m_ref, w_ref, x_ref, g_ref, b_ref, h_ref, hb_ref, *, alpha):
    mix = jnp.dot(m_ref[...], w_ref[...], preferred_element_type=F32)
    h = _layer_norm(alpha * x_ref[...] + mix, g_ref[...], b_ref[...])
    h_ref[...] = h
    hb_ref[...] = h.astype(BF16)


def _out_ln(merged_bf, w_out_bf, x, g, b, alpha, tm):
    m = x.shape[0]
    row = pl.BlockSpec((tm, D_MODEL), lambda i: (i, 0))
    vec = pl.BlockSpec((1, D_MODEL), lambda i: (0, 0))
    return pl.pallas_call(
        functools.partial(_out_ln_kernel, alpha=alpha),
        grid=(m // tm,),
        in_specs=[row, pl.BlockSpec((D_MODEL, D_MODEL), lambda i: (0, 0)), row, vec, vec],
        out_specs=[row, row],
        out_shape=[jax.ShapeDtypeStruct((m, D_MODEL), F32), jax.ShapeDtypeStruct((m, D_MODEL), BF16)],
        compiler_params=_params("arbitrary"),
        name="out_ln",
    )(merged_bf, w_out_bf, x, g.reshape(1, D_MODEL), b.reshape(1, D_MODEL))


def _staircase_pairs(k):
    return [(a, b) for a in range(k) for b in range(k) if (a + 1) * (b + 1) <= k]


def _kth_largest(vals, k):
    neg = -jnp.inf
    cur = functools.reduce(jnp.maximum, vals)
    for _ in range(k - 1):
        nxt = None
        for v in vals:
            c = jnp.where(v < cur, v, neg)
            nxt = c if nxt is None else jnp.maximum(nxt, c)
        cur = nxt
    return cur


def _peer_route_kernel(h_ref, wq_ref, keys_ref, s0_ref, s1_ref, e0_ref, e1_ref, tau_ref,
                       q_s, top_s, *, tm):
    q_s[...] = jnp.dot(h_ref[...], wq_ref[...], preferred_element_type=F32).astype(BF16)
    neg = -jnp.inf
    for hd in range(PEER_HEADS):
        for c in range(2):
            col = (hd * 2 + c) * PEER_HALF
            s = lax.dot_general(keys_ref[hd, c], q_s[:, col:col + PEER_HALF], NT_DIMS,
                                preferred_element_type=F32)
            (s0_ref if c == 0 else s1_ref)[hd] = s
            cur = jnp.max(s, axis=0, keepdims=True)
            top_s[c, 0, hd:hd + 1, :] = cur
            for a in range(1, PEER_TOPK):
                cur = jnp.max(jnp.where(s < cur, s, neg), axis=0, keepdims=True)
                top_s[c, a, hd:hd + 1, :] = cur
    pairs = _staircase_pairs(PEER_TOPK)
    sums = [top_s[0, a] + top_s[1, b] for a, b in pairs]
    tau = _kth_largest(sums, PEER_TOPK)
    best = top_s[0, 0] + top_s[1, 0]
    z = None
    for v in sums:
        e = jnp.where(v >= tau, jnp.exp(v - best), 0.0)
        z = e if z is None else z + e
    tau_ref[...] = tau
    inv_z = 1.0 / z
    for hd in range(PEER_HEADS):
        e0_ref[hd] = jnp.exp(s0_ref[hd] - top_s[0, 0, hd:hd + 1, :]) * inv_z[hd:hd + 1, :]
        e1_ref[hd] = jnp.exp(s1_ref[hd] - top_s[1, 0, hd:hd + 1, :])


def _peer_route(h_bf, wq_bf, keys_bf, tm):
    m = h_bf.shape[0]
    tab = pl.BlockSpec((PEER_HEADS, PEER_KEYS, tm), lambda i: (0, 0, i))
    tab_shape = jax.ShapeDtypeStruct((PEER_HEADS, PEER_KEYS, m), F32)
    return pl.pallas_call(
        functools.partial(_peer_route_kernel, tm=tm),
        grid=(m // tm,),
        in_specs=[pl.BlockSpec((tm, D_MODEL), lambda i: (i, 0)),
                  pl.BlockSpec((D_MODEL, D_MODEL), lambda i: (0, 0)),
                  pl.BlockSpec((PEER_HEADS, 2, PEER_KEYS, PEER_HALF), lambda i: (0, 0, 0, 0))],
        out_specs=[tab, tab, tab, tab, pl.BlockSpec((PEER_HEADS, tm), lambda i: (0, i))],
        out_shape=[tab_shape, tab_shape, tab_shape, tab_shape,
                   jax.ShapeDtypeStruct((PEER_HEADS, m), F32)],
        scratch_shapes=[pltpu.VMEM((tm, D_MODEL), BF16),
                        pltpu.VMEM((2, PEER_TOPK, PEER_HEADS, tm), F32)],
        compiler_params=_params("arbitrary"),
        name="peer_route",
    )(h_bf, wq_bf, keys_bf)


def _peer_expert_kernel(h_ref, u_ref, vt_ref, s0_ref, s1_ref, e0_ref, e1_ref, tau_ref, o_ref,
                        p_s, *, eb, n_eb):
    e = pl.program_id(1)
    cur = e % 2
    prev = 1 - cur

    @pl.when(e == 0)
    def _():
        o_ref[...] = jnp.zeros(o_ref.shape, F32)
        p_s[1] = jnp.zeros(p_s.shape[1:], BF16)

    o_ref[...] += jnp.dot(vt_ref[...], p_s[prev], preferred_element_type=F32)

    blk = jnp.minimum(e, n_eb - 1)
    ht = lax.dot_general(u_ref[...], h_ref[...], NT_DIMS, preferred_element_type=F32)
    for ii in range(eb // PEER_KEYS):
        i = blk * (eb // PEER_KEYS) + ii
        w = None
        for hd in range(PEER_HEADS):
            s0 = s0_ref[hd, pl.ds(i, 1), :]
            e0 = e0_ref[hd, pl.ds(i, 1), :]
            tau = tau_ref[hd:hd + 1, :]
            sel = jnp.where(s0 + s1_ref[hd] >= tau, e0 * e1_ref[hd], 0.0)
            w = sel if w is None else w + sel
        rows = slice(ii * PEER_KEYS, (ii + 1) * PEER_KEYS)
        p_s[cur, rows, :] = (_gelu_tanh(ht[rows, :]) * w).astype(BF16)


def _peer_experts(h_bf, u_bf, vt_bf, route, tb, eb):
    m = h_bf.shape[0]
    n_eb = u_bf.shape[0] // eb
    s0, s1, e0, e1, tau = route
    tab = pl.BlockSpec((PEER_HEADS, PEER_KEYS, tb), lambda t, e: (0, 0, t))
    return pl.pallas_call(
        functools.partial(_peer_expert_kernel, eb=eb, n_eb=n_eb),
        grid=(m // tb, n_eb + 1),
        in_specs=[pl.BlockSpec((tb, D_MODEL), lambda t, e: (t, 0)),
                  pl.BlockSpec((eb, D_MODEL), lambda t, e: (jnp.minimum(e, n_eb - 1), 0)),
                  pl.BlockSpec((D_MODEL, eb), lambda t, e: (0, jnp.maximum(e - 1, 0))),
                  tab, tab, tab, tab,
                  pl.BlockSpec((PEER_HEADS, tb), lambda t, e: (0, t))],
        out_specs=pl.BlockSpec((D_MODEL, tb), lambda t, e: (0, t)),
        out_shape=jax.ShapeDtypeStruct((D_MODEL, m), F32),
        scratch_shapes=[pltpu.VMEM((2, eb, tb), BF16)],
        compiler_params=_params("arbitrary", "arbitrary"),
        name="peer_experts",
    )(h_bf, u_bf, vt_bf, s0, s1, e0, e1, tau)


def _ffn_ln_kernel(ft_ref, h_ref, g_ref, b_ref, y_ref, yb_ref, *, alpha):
    y = _layer_norm(alpha * h_ref[...] + ft_ref[...].T, g_ref[...], b_ref[...])
    y_ref[...] = y
    yb_ref[...] = y.astype(BF16)


def _ffn_ln(ffn_t, h, g, b, alpha, tm):
    m = h.shape[0]
    row = pl.BlockSpec((tm, D_MODEL), lambda i: (i, 0))
    vec = pl.BlockSpec((1, D_MODEL), lambda i: (0, 0))
    return pl.pallas_call(
        functools.partial(_ffn_ln_kernel, alpha=alpha),
        grid=(m // tm,),
        in_specs=[pl.BlockSpec((D_MODEL, tm), lambda i: (0, i)), row, vec, vec],
        out_specs=[row, row],
        out_shape=[jax.ShapeDtypeStruct((m, D_MODEL), F32), jax.ShapeDtypeStruct((m, D_MODEL), BF16)],
        compiler_params=_params("arbitrary"),
        name="ffn_ln",
    )(ffn_t, h, g.reshape(1, D_MODEL), b.reshape(1, D_MODEL))


def _row_tile(m, pref):
    t = min(m, pref)
    assert m % t == 0, (m, t)
    return t


def _token_mix_tail(x, x_bf, a_out, b_out, c_out, w, lp, alpha):
    m = x.shape[0]
    merged = _merge(x_bf, a_out, b_out, c_out, w['w_in'], w['w_branch'],
                    _row_tile(m, 512), 512)
    h, h_bf = _out_ln(merged, w['w_out'], x, lp['ln1_g'], lp['ln1_b'], alpha, _row_tile(m, 256))
    route = _peer_route(h_bf, w['peer_w_q'], w['peer_keys'], _row_tile(m, 256))
    ffn_t = _peer_experts(h_bf, w['peer_u'], w['peer_vt'], route, _row_tile(m, 512), 512)
    return _ffn_ln(ffn_t, h, lp['ln2_g'], lp['ln2_b'], alpha, _row_tile(m, 256))


def _prompt_layer(x, x_bf, lp, w, ssm, rope_tabs, bsz, t, lambda_init, alpha):
    proj = _project(x_bf, w['w_in'], rope_tabs, _row_tile(bsz * t, 512))
    a_out = _sb_prompt(proj, bsz, t, _row_tile(t, 256))
    c_out = _da_prompt(proj, lp, bsz, t, _row_tile(t, 256), lambda_init)
    nseg = 8 // bsz if 8 % bsz == 0 and t % (8 // bsz) == 0 else 1
    assert (bsz * nseg) % 8 == 0
    seg_len = t // nseg
    b_out, h_re, h_im = _s5_prompt(proj[3], ssm, bsz, t, nseg, _row_tile(seg_len, 64))
    y, y_bf = _token_mix_tail(x, x_bf, a_out, b_out, c_out, w, lp, alpha)
    state = (proj[1], proj[2], proj[5], proj[6], h_re, h_im)
    return y, y_bf, state


def _sample_layer(x, x_bf, lp, w, ssm, rope_tabs, caches, layer, page_table, h0_re, h0_im,
                  nb, t, lambda_init, alpha):
    proj = _project(x_bf, w['w_in'], rope_tabs, nb * t)
    a_out, c_out = _sample_attn(proj, caches, layer, page_table, lp, nb, t, lambda_init)
    b_out, h_re, h_im = _s5_sample(proj[3], ssm, h0_re, h0_im, nb, t)
    y, y_bf = _token_mix_tail(x, x_bf, a_out, b_out, c_out, w, lp, alpha)
    state = (proj[1], proj[2], proj[5], proj[6], h_re, h_im)
    return y, y_bf, state


def kernel(x_prompt, x_sample, cache_sb_k, cache_sb_v, cache_da_k, cache_da_v, state_ssm_re, state_ssm_im, page_table, w_in, w_branch, w_out, ssm_lam_re, ssm_lam_im, ssm_log_dt, ssm_b_re, ssm_b_im, ssm_c_re, ssm_c_im, ssm_d, ssm_w_glu, da_lam_q1, da_lam_k1, da_lam_q2, da_lam_k2, da_norm_g, ln1_g, ln1_b, peer_w_q, peer_keys, peer_u, peer_v, ln2_g, ln2_b):
    depth = w_in.shape[0]
    bsz, t, _ = x_prompt.shape
    nb, ts, _ = x_sample.shape
    n_past = page_table.shape[1] * PAGE_SIZE
    alpha = (2 * depth) ** 0.25
    n_pool = cache_sb_k.shape[1]
    caches = (cache_sb_k.reshape(depth, n_pool, PAGE_ROWS, 128),
              cache_sb_v.reshape(depth, n_pool, PAGE_ROWS, 128),
              jnp.transpose(cache_da_k, (0, 1, 3, 4, 5, 2)).reshape(depth, n_pool, PAGE_ROWS, PAGE_SIZE),
              cache_da_v.reshape(depth, n_pool, PAGE_ROWS, 128))
    rope_p = _rope_tables(jnp.tile(jnp.arange(t, dtype=jnp.int32), bsz))
    rope_s = _rope_tables(jnp.tile(n_past + jnp.arange(ts, dtype=jnp.int32), nb))

    yp = x_prompt.reshape(bsz * t, D_MODEL)
    ys = x_sample.reshape(nb * ts, D_MODEL)
    yp_bf = yp.astype(BF16)
    ys_bf = ys.astype(BF16)
    rows_p, rows_s = [], []
    for l in range(depth):
        lp = {'ssm_lam_re': ssm_lam_re[l], 'ssm_lam_im': ssm_lam_im[l], 'ssm_log_dt': ssm_log_dt[l],
              'ssm_b_re': ssm_b_re[l], 'ssm_b_im': ssm_b_im[l],
              'ssm_c_re': ssm_c_re[l], 'ssm_c_im': ssm_c_im[l],
              'ssm_d': ssm_d[l], 'ssm_w_glu': ssm_w_glu[l],
              'da_lam_q1': da_lam_q1[l], 'da_lam_k1': da_lam_k1[l],
              'da_lam_q2': da_lam_q2[l], 'da_lam_k2': da_lam_k2[l], 'da_norm_g': da_norm_g[l],
              'ln1_g': ln1_g[l], 'ln1_b': ln1_b[l], 'ln2_g': ln2_g[l], 'ln2_b': ln2_b[l]}
        w = {'w_in': w_in[l].astype(BF16), 'w_branch': w_branch[l].astype(BF16),
             'w_out': w_out[l].astype(BF16), 'peer_w_q': peer_w_q[l].astype(BF16),
             'peer_keys': peer_keys[l].astype(BF16), 'peer_u': peer_u[l].astype(BF16),
             'peer_vt': peer_v[l].astype(BF16).T}
        ssm = _ssm_params(lp)
        lambda_init = 0.8 - 0.6 * math.exp(-0.3 * l)
        yp, yp_bf, st_p = _prompt_layer(yp, yp_bf, lp, w, ssm, rope_p, bsz, t, lambda_init, alpha)
        ys, ys_bf, st_s = _sample_layer(ys, ys_bf, lp, w, ssm, rope_s, caches, l, page_table,
                                        state_ssm_re[l], state_ssm_im[l], nb, ts, lambda_init, alpha)
        rows_p.append(st_p)
        rows_s.append(st_s)

    def pack(rows, lead):
        sk, sv, dk, dv, hr, hi = [jnp.stack(z) for z in zip(*rows)]
        n = lead[0]
        return (sk.reshape(depth, *lead, SB_HEADS, SB_HEAD_DIM),
                sv.reshape(depth, *lead, SB_HEADS, SB_HEAD_DIM),
                dk.reshape(depth, *lead, DA_HEADS, 2, DA_SUB_DIM),
                dv.reshape(depth, *lead, DA_HEADS, DA_V_DIM),
                hr.reshape(depth, n, SSM_GROUPS, SSM_STATE),
                hi.reshape(depth, n, SSM_GROUPS, SSM_STATE))

    return ((yp.reshape(bsz, t, D_MODEL), ys.reshape(nb, ts, D_MODEL))
            + pack(rows_p, (bsz, t)) + pack(rows_s, (nb, ts)))
```

```python
import functools
import math

import jax
import jax.numpy as jnp
from jax import lax
from jax.experimental import pallas as pl
from jax.experimental.pallas import tpu as pltpu

F32 = jnp.float32
BF16 = jnp.bfloat16

D_MODEL = 2048
PAGE_SIZE = 128
SB_HEADS = 4
SB_HEAD_DIM = 128
SSM_GROUP = 16
SSM_GROUPS = 32
SSM_STATE = 64
SSM_WIDTH = SSM_GROUPS * SSM_GROUP
SSM_LANES = SSM_GROUPS * SSM_STATE
DA_HEADS = 4
DA_SUB_DIM = 64
DA_V_DIM = 128
ROPE_THETA = 500000.0
ROPE_DIM = DA_SUB_DIM // 4
N_BRANCH = 3
BRANCH_WIDTH = 512
N_PROJ_GROUPS = 7
GATE_COL0 = N_PROJ_GROUPS * BRANCH_WIDTH
PEER_HEADS = 8
PEER_KEYS = 128
PEER_HALF = 128
PEER_TOPK = 16
LN_EPS = 1e-5

V7X_VMEM_BYTES = 64 * 1024 * 1024
VMEM_LIMIT = V7X_VMEM_BYTES - 8 * 1024 * 1024

NT_DIMS = (((1,), (1,)), ((), ()))

F32_EXP_UNDERFLOW = -105.0


def _params(*sem):
    return pltpu.CompilerParams(dimension_semantics=sem, vmem_limit_bytes=VMEM_LIMIT)


def _gelu_tanh(x):
    return 0.5 * x * (1.0 + jnp.tanh(math.sqrt(2.0 / math.pi) * (x + 0.044715 * (x * x * x))))


def _sigmoid(x):
    return 1.0 / (1.0 + jnp.exp(-x))


def _proj_kernel(x_ref, w_ref, cos_ref, sa_ref, sb_ref, o_ref):
    j = pl.program_id(0)
    o_ref[0] = jnp.dot(x_ref[...], w_ref[...], preferred_element_type=F32)

    @pl.when((j == 4) | (j == 5))
    def _():
        c = cos_ref[...]
        sa = sa_ref[...]
        sb = sb_ref[...]
        for t in range(BRANCH_WIDTH // 128):
            xt = o_ref[0, :, t * 128:(t + 1) * 128]
            o_ref[0, :, t * 128:(t + 1) * 128] = (
                xt * c + pltpu.roll(xt, 8, 1) * sa + pltpu.roll(xt, 120, 1) * sb)


def _project(x_bf, w_in_bf, rope_tabs, tm):
    m = x_bf.shape[0]
    nr = m // tm
    cos_t, sa_t, sb_t = rope_tabs

    def tab_map(j, i):
        return (jnp.where((j == 4) | (j == 5), i, 0), 0)

    return pl.pallas_call(
        _proj_kernel,
        grid=(N_PROJ_GROUPS, nr),
        in_specs=[
            pl.BlockSpec((tm, D_MODEL), lambda j, i: (i, 0)),
            pl.BlockSpec((D_MODEL, BRANCH_WIDTH), lambda j, i: (0, j)),
            pl.BlockSpec((tm, 128), tab_map),
            pl.BlockSpec((tm, 128), tab_map),
            pl.BlockSpec((tm, 128), tab_map),
        ],
        out_specs=pl.BlockSpec((1, tm, BRANCH_WIDTH), lambda j, i: (j, i, 0)),
        out_shape=jax.ShapeDtypeStruct((N_PROJ_GROUPS, m, BRANCH_WIDTH), F32),
        compiler_params=_params("arbitrary", "arbitrary"),
        name="proj",
    )(x_bf, w_in_bf, cos_t, sa_t, sb_t)


def _rope_tables(pos):
    half = ROPE_DIM // 2
    inv_freq = ROPE_THETA ** (-jnp.arange(half, dtype=F32) / half)
    ang = pos.astype(F32)[:, None] * inv_freq[None, :]
    cos = jnp.cos(ang)
    sin = jnp.sin(ang)
    n = pos.shape[0]
    ones = jnp.ones((n, DA_SUB_DIM - ROPE_DIM), F32)
    zeros = jnp.zeros((n, DA_SUB_DIM - ROPE_DIM), F32)
    zh = jnp.zeros((n, half), F32)
    c64 = jnp.concatenate([cos, cos, ones], axis=1)
    sa64 = jnp.concatenate([zh, sin, zeros], axis=1)
    sb64 = jnp.concatenate([-sin, zh, zeros], axis=1)
    return tuple(jnp.concatenate([t, t], axis=1) for t in (c64, sa64, sb64))


def _log_sigmoid(z):
    return jnp.minimum(z, 0.0) - jnp.log1p(jnp.exp(-jnp.abs(z)))


def _split_bf16(x):
    hi = x.astype(BF16)
    lo = (x - hi.astype(F32)).astype(BF16)
    return hi, lo


def _sb_block(q, kb, vb, tri, c, acc, scale, mask):
    tk = kb.shape[0]
    z = lax.dot_general(q, kb, NT_DIMS, preferred_element_type=F32) * scale
    lb = _log_sigmoid(z)
    lk = lb - z
    if mask is not None:
        lk = jnp.where(mask, lk, 0.0)
    hi, lo = _split_bf16(lk)
    r = (jnp.dot(hi, tri, preferred_element_type=F32)
         + jnp.dot(lo, tri, preferred_element_type=F32))
    w = jnp.exp(lb + r[:, :tk] + c)
    if mask is not None:
        w = jnp.where(mask, w, 0.0)
    acc = acc + jnp.dot(w.astype(BF16), vb, preferred_element_type=F32)
    return c + r[:, tk:], acc


def _sb_prompt_kernel(q_ref, k_ref, v_ref, tri_ref, o_ref, kbf, vbf, *, blk, scale):
    qi = pl.program_id(2)

    @pl.when(qi == 0)
    def _():
        kbf[...] = k_ref[0].astype(BF16)
        vbf[...] = v_ref[0].astype(BF16)

    q = q_ref[0].astype(BF16)
    tri = tri_ref[...]
    row = lax.broadcasted_iota(jnp.int32, (blk, blk), 0)
    col = lax.broadcasted_iota(jnp.int32, (blk, blk), 1)
    off = pl.multiple_of(qi * blk, blk)
    c0 = jnp.zeros((blk, blk), F32)
    a0 = jnp.zeros((blk, SB_HEAD_DIM), F32)
    c, acc = _sb_block(q, kbf[pl.ds(off, blk), :], vbf[pl.ds(off, blk), :], tri, c0, a0,
                       scale, col < row)

    def cond(carry):
        jj, _, _, cmax = carry
        return (jj < qi) & (cmax > F32_EXP_UNDERFLOW)

    def body(carry):
        jj, c, acc, _ = carry
        o = pl.multiple_of((qi - 1 - jj) * blk, blk)
        c, acc = _sb_block(q, kbf[pl.ds(o, blk), :], vbf[pl.ds(o, blk), :], tri, c, acc, scale, None)
        return jj + 1, c, acc, jnp.max(c[:, :128])

    _, _, acc, _ = lax.while_loop(cond, body, (jnp.int32(0), c, acc, jnp.max(c[:, :128])))
    o_ref[...] = acc


def _tri_ones(blk):
    j = jnp.arange(blk)[:, None]
    s = jnp.arange(blk)[None, :]
    return jnp.concatenate([(j > s).astype(BF16), jnp.ones((blk, blk), BF16)], axis=1)


def _sb_prompt(proj, bsz, t, blk):
    nq = t // blk
    kern = functools.partial(_sb_prompt_kernel, blk=blk, scale=SB_HEAD_DIM ** -0.5)
    return pl.pallas_call(
        kern,
        grid=(bsz, SB_HEADS, nq),
        in_specs=[
            pl.BlockSpec((1, blk, SB_HEAD_DIM), lambda b, h, i: (0, b * nq + i, h)),
            pl.BlockSpec((1, t, SB_HEAD_DIM), lambda b, h, i: (1, b, h)),
            pl.BlockSpec((1, t, SB_HEAD_DIM), lambda b, h, i: (2, b, h)),
            pl.BlockSpec((blk, 2 * blk), lambda b, h, i: (0, 0)),
        ],
        out_specs=pl.BlockSpec((blk, SB_HEAD_DIM), lambda b, h, i: (b * nq + i, h)),
        out_shape=jax.ShapeDtypeStruct((bsz * t, SB_HEADS * SB_HEAD_DIM), F32),
        scratch_shapes=[pltpu.VMEM((t, SB_HEAD_DIM), BF16), pltpu.VMEM((t, SB_HEAD_DIM), BF16)],
        compiler_params=_params("arbitrary", "arbitrary", "arbitrary"),
        name="sb_prompt",
    )(proj, proj, proj, _tri_ones(blk))


def _da_lambda(lq1, lk1, lq2, lk2, h, lambda_init):
    a = jnp.sum(lq1[pl.ds(h, 1), :] * lk1[pl.ds(h, 1), :], axis=1, keepdims=True)
    b = jnp.sum(lq2[pl.ds(h, 1), :] * lk2[pl.ds(h, 1), :], axis=1, keepdims=True)
    return jnp.exp(a) - jnp.exp(b) + lambda_init


def _split_components(q):
    lane = lax.broadcasted_iota(jnp.int32, q.shape, 1)
    q0 = jnp.where(lane < DA_SUB_DIM, q, 0.0).astype(BF16)
    q1 = jnp.where(lane >= DA_SUB_DIM, q, 0.0).astype(BF16)
    return q0, q1


def _softmax_block(qc, kb, vb, m, l, acc, scale, mask):
    s = lax.dot_general(qc, kb, NT_DIMS, preferred_element_type=F32) * scale
    if mask is not None:
        s = jnp.where(mask, s, -jnp.inf)
    m_new = jnp.maximum(m, jnp.max(s, axis=1, keepdims=True))
    p = jnp.exp(s - m_new)
    alpha = jnp.exp(m - m_new)
    l = alpha * l + jnp.sum(p, axis=1, keepdims=True)
    acc = alpha * acc + jnp.dot(p.astype(BF16), vb, preferred_element_type=F32)
    return m_new, l, acc


def _da_finish(st0, st1, lam, g, lambda_init):
    o = st0[2] / st0[1] - lam * (st1[2] / st1[1])
    o = o * lax.rsqrt(jnp.mean(o * o, axis=-1, keepdims=True) + LN_EPS) * g
    return (1.0 - lambda_init) * o


def _da_prompt_kernel(q_ref, k_ref, v_ref, lq1, lk1, lq2, lk2, g_ref, o_ref, kbf, vbf,
                      *, blk, scale, lambda_init):
    h = pl.program_id(1)
    qi = pl.program_id(2)

    @pl.when(qi == 0)
    def _():
        kbf[...] = k_ref[0].astype(BF16)
        vbf[...] = v_ref[0].astype(BF16)

    q0, q1 = _split_components(q_ref[0])
    row = lax.broadcasted_iota(jnp.int32, (blk, blk), 0)
    col = lax.broadcasted_iota(jnp.int32, (blk, blk), 1)
    mask = col <= row
    off = pl.multiple_of(qi * blk, blk)
    init = (jnp.full((blk, 1), -jnp.inf, F32), jnp.zeros((blk, 1), F32),
            jnp.zeros((blk, DA_V_DIM), F32))
    kd = kbf[pl.ds(off, blk), :]
    vd = vbf[pl.ds(off, blk), :]
    st0 = _softmax_block(q0, kd, vd, *init, scale, mask)
    st1 = _softmax_block(q1, kd, vd, *init, scale, mask)

    def body(j, carry):
        o = pl.multiple_of(j * blk, blk)
        kb = kbf[pl.ds(o, blk), :]
        vb = vbf[pl.ds(o, blk), :]
        return (_softmax_block(q0, kb, vb, *carry[0], scale, None),
                _softmax_block(q1, kb, vb, *carry[1], scale, None))

    st0, st1 = lax.fori_loop(0, qi, body, (st0, st1))
    lam = _da_lambda(lq1, lk1, lq2, lk2, h, lambda_init)
    o_ref[...] = _da_finish(st0, st1, lam, g_ref[...], lambda_init)


def _da_prompt(proj, lp, bsz, t, blk, lambda_init):
    nq = t // blk
    kern = functools.partial(_da_prompt_kernel, blk=blk, scale=DA_SUB_DIM ** -0.5,
                             lambda_init=lambda_init)
    lam_spec = pl.BlockSpec((DA_HEADS, DA_SUB_DIM), lambda b, h, i: (0, 0))
    return pl.pallas_call(
        kern,
        grid=(bsz, DA_HEADS, nq),
        in_specs=[
            pl.BlockSpec((1, blk, 128), lambda b, h, i: (4, b * nq + i, h)),
            pl.BlockSpec((1, t, 128), lambda b, h, i: (5, b, h)),
            pl.BlockSpec((1, t, DA_V_DIM), lambda b, h, i: (6, b, h)),
            lam_spec, lam_spec, lam_spec, lam_spec,
            pl.BlockSpec((1, DA_V_DIM), lambda b, h, i: (0, 0)),
        ],
        out_specs=pl.BlockSpec((blk, DA_V_DIM), lambda b, h, i: (b * nq + i, h)),
        out_shape=jax.ShapeDtypeStruct((bsz * t, DA_HEADS * DA_V_DIM), F32),
        scratch_shapes=[pltpu.VMEM((t, 128), BF16), pltpu.VMEM((t, DA_V_DIM), BF16)],
        compiler_params=_params("arbitrary", "arbitrary", "arbitrary"),
        name="da_prompt",
    )(proj, proj, proj, lp['da_lam_q1'], lp['da_lam_k1'], lp['da_lam_q2'], lp['da_lam_k2'],
      lp['da_norm_g'].reshape(1, DA_V_DIM))


SAMPLE_ROWS = 8
PAGES_PER_STEP = 8
PAGE_ROWS = PAGE_SIZE * SB_HEADS


def _tri_t_aug():
    s = jnp.arange(PAGE_SIZE + 8)[:, None]
    j = jnp.arange(PAGE_SIZE)[None, :]
    return ((j > s) | (s >= PAGE_SIZE)).astype(BF16)


def _head_rows(refs, h, lead):
    idx = (pl.ds(h, PAGE_SIZE, stride=SB_HEADS), slice(None))
    parts = [(ref[(0,) + idx] if lead else ref[idx]).astype(BF16) for ref in refs]
    return parts[0] if len(parts) == 1 else jnp.concatenate(parts, axis=0)


def _sb_pages(qt_ref, k_refs, v_refs, lead, tri_t, c, acc, mask):
    z = None
    for h in range(SB_HEADS):
        d = jnp.dot(_head_rows(k_refs, h, lead), qt_ref[0, h * 128:(h + 1) * 128, :],
                    preferred_element_type=F32)
        z = d if z is None else z + d
    z = z * (SB_HEAD_DIM ** -0.5)
    lb = _log_sigmoid(z)
    lk = lb - z
    if mask is not None:
        lk = jnp.where(mask, lk, 0.0)
    hi, lo = _split_bf16(lk)
    later = []
    for k in range(len(k_refs)):
        rows = slice(k * PAGE_SIZE, (k + 1) * PAGE_SIZE)
        r = (jnp.dot(tri_t, hi[rows], preferred_element_type=F32)
             + jnp.dot(tri_t, lo[rows], preferred_element_type=F32))
        later.append(r[:PAGE_SIZE] + c[0:1, :])
        c = c + r[PAGE_SIZE:]
    later = later[0] if len(later) == 1 else jnp.concatenate(later, axis=0)
    w = jnp.exp(lb + later)
    if mask is not None:
        w = jnp.where(mask, w, 0.0)
    wt = w.T
    parts = []
    for h in range(SB_HEADS):
        rows = slice(h * SAMPLE_ROWS, (h + 1) * SAMPLE_ROWS)
        parts.append(acc[rows] + jnp.dot(wt[rows].astype(BF16), _head_rows(v_refs, h, lead),
                                         preferred_element_type=F32))
    return c, jnp.concatenate(parts, axis=0)


def _da_pages(qs_ref, kt_refs, v_refs, lead, m, l, acc, mask):
    scale = DA_SUB_DIM ** -0.5
    nr = 2 * SAMPLE_ROWS
    ms, ls, accs = [], [], []
    for h in range(DA_HEADS):
        rows = slice(h * nr, (h + 1) * nr)
        hs = slice(h * 128, (h + 1) * 128)
        kts = [(ref[0, hs, :] if lead else ref[hs, :]).astype(BF16) for ref in kt_refs]
        kt = kts[0] if len(kts) == 1 else jnp.concatenate(kts, axis=1)
        s = jnp.dot(qs_ref[0, h], kt, preferred_element_type=F32) * scale
        if mask is not None:
            s = jnp.where(mask, s, -jnp.inf)
        m_old = m[rows]
        m_new = jnp.maximum(m_old, jnp.max(s, axis=1, keepdims=True))
        alpha = jnp.exp(m_old - m_new)
        p = jnp.exp(s - m_new[:, :1])
        ms.append(m_new)
        ls.append(alpha * l[rows] + jnp.sum(p, axis=1, keepdims=True))
        accs.append(alpha * acc[rows] + jnp.dot(p.astype(BF16), _head_rows(v_refs, h, lead),
                                                preferred_element_type=F32))
    return (jnp.concatenate(ms, axis=0), jnp.concatenate(ls, axis=0), jnp.concatenate(accs, axis=0))


def _sample_attn_kernel(pt_ref, qt_ref, qs_ref, nk_ref, nv_ref, dnkt_ref, dnv_ref, *rest,
                        n_new, lambda_init):
    del pt_ref
    npp = PAGES_PER_STEP
    cache_refs = rest[:4 * npp]
    tri_ref, lq1, lk1, lq2, lk2, g_ref, oa_ref, oc_ref, c_s, acc_s, m_s, l_s, dacc_s = rest[4 * npp:]
    s = pl.program_id(1)
    tri_t = tri_ref[...]
    nr = 2 * SAMPLE_ROWS

    @pl.when(s == 0)
    def _():
        key = lax.broadcasted_iota(jnp.int32, (PAGE_SIZE, 128), 0)
        qry = lax.broadcasted_iota(jnp.int32, (PAGE_SIZE, 128), 1) % SAMPLE_ROWS
        c, acc = _sb_pages(qt_ref, [nk_ref], [nv_ref], True, tri_t,
                           jnp.zeros((SAMPLE_ROWS, 128), F32),
                           jnp.zeros((SB_HEADS * SAMPLE_ROWS, 128), F32),
                           (key < n_new) & (key < qry))
        c_s[...] = c
        acc_s[...] = acc
        qry = lax.broadcasted_iota(jnp.int32, (nr, PAGE_SIZE), 0) % SAMPLE_ROWS
        key = lax.broadcasted_iota(jnp.int32, (nr, PAGE_SIZE), 1)
        m, l, dacc = _da_pages(qs_ref, [dnkt_ref], [dnv_ref], True,
                               jnp.full((DA_HEADS * nr, 128), -jnp.inf, F32),
                               jnp.zeros((DA_HEADS * nr, 128), F32),
                               jnp.zeros((DA_HEADS * nr, 128), F32),
                               (key < n_new) & (key <= qry))
        m_s[...] = m
        l_s[...] = l
        dacc_s[...] = dacc

    c, acc = c_s[...], acc_s[...]
    m, l, dacc = m_s[...], l_s[...], dacc_s[...]
    c, acc = _sb_pages(qt_ref, cache_refs[0::4], cache_refs[1::4], False, tri_t, c, acc, None)
    m, l, dacc = _da_pages(qs_ref, cache_refs[2::4], cache_refs[3::4], False, m, l, dacc, None)
    c_s[...] = c
    acc_s[...] = acc
    m_s[...] = m
    l_s[...] = l
    dacc_s[...] = dacc

    @pl.when(s == pl.num_programs(1) - 1)
    def _():
        oa_ref[0] = acc
        for h in range(DA_HEADS):
            lam = _da_lambda(lq1, lk1, lq2, lk2, h, lambda_init)
            r0 = slice(h * nr, h * nr + SAMPLE_ROWS)
            r1 = slice(h * nr + SAMPLE_ROWS, (h + 1) * nr)
            oc_ref[0, h] = _da_finish((None, l[r0], dacc[r0]), (None, l[r1], dacc[r1]),
                                      lam, g_ref[...], lambda_init)


def _sample_attn(proj, caches, layer, page_table, lp, nb, ts, lambda_init):
    n_pages = page_table.shape[1]
    npp = PAGES_PER_STEP
    assert n_pages % npp == 0 and ts <= SAMPLE_ROWS
    pad_q = ((0, 0), (0, SAMPLE_ROWS - ts), (0, 0), (0, 0))

    q = jnp.pad(proj[0].reshape(nb, ts, SB_HEADS, SB_HEAD_DIM), pad_q)
    qt = jnp.einsum('bihd,hg->bhdgi', q, jnp.eye(SB_HEADS, dtype=F32))
    qt = qt.reshape(nb, SB_HEADS * SB_HEAD_DIM, SB_HEADS * SAMPLE_ROWS)
    qt = jnp.pad(qt, ((0, 0), (0, 0), (0, 128 - SB_HEADS * SAMPLE_ROWS))).astype(BF16)
    dq = jnp.pad(proj[4].reshape(nb, ts, DA_HEADS, 128), pad_q).transpose(0, 2, 1, 3)
    comp = (jnp.arange(128) // DA_SUB_DIM)[None, :] == jnp.arange(2)[:, None]
    qs = jnp.where(comp[None, None, :, None, :], dq[:, :, None], 0.0)
    qs = qs.reshape(nb, DA_HEADS, 2 * SAMPLE_ROWS, 128).astype(BF16)

    def new_page(g):
        return jnp.pad(proj[g].reshape(nb, ts * SB_HEADS, 128),
                       ((0, 0), (0, PAGE_ROWS - ts * SB_HEADS), (0, 0)))

    dnkt = jnp.pad(proj[5].reshape(nb, ts, 512).transpose(0, 2, 1),
                   ((0, 0), (0, 0), (0, PAGE_SIZE - ts)))

    kern = functools.partial(_sample_attn_kernel, n_new=ts, lambda_init=lambda_init)
    per_seq = lambda b, s, pt: (b, 0, 0)
    page_spec = pl.BlockSpec((1, PAGE_ROWS, 128), per_seq)

    def cache_spec(k):
        return pl.BlockSpec((None, None, PAGE_ROWS, 128),
                            lambda b, s, pt: (layer, pt[b, n_pages - 1 - (s * npp + k)], 0, 0))

    const2 = lambda b, s, pt: (0, 0)
    lam_spec = pl.BlockSpec((DA_HEADS, DA_SUB_DIM), const2)
    cache_specs, cache_args = [], []
    for k in range(npp):
        cache_specs += [cache_spec(k)] * 4
        cache_args += list(caches)
    n_state = DA_HEADS * 2 * SAMPLE_ROWS
    grid_spec = pltpu.PrefetchScalarGridSpec(
        num_scalar_prefetch=1,
        grid=(nb, n_pages // npp),
        in_specs=[pl.BlockSpec((1, SB_HEADS * SB_HEAD_DIM, 128), per_seq),
                  pl.BlockSpec((1, DA_HEADS, 2 * SAMPLE_ROWS, 128), lambda b, s, pt: (b, 0, 0, 0)),
                  page_spec, page_spec, page_spec, page_spec]
                 + cache_specs
                 + [pl.BlockSpec((PAGE_SIZE + 8, PAGE_SIZE), const2),
                    lam_spec, lam_spec, lam_spec, lam_spec,
                    pl.BlockSpec((1, DA_V_DIM), const2)],
        out_specs=[pl.BlockSpec((1, SB_HEADS * SAMPLE_ROWS, 128), per_seq),
                   pl.BlockSpec((1, DA_HEADS, SAMPLE_ROWS, 128), lambda b, s, pt: (b, 0, 0, 0))],
        scratch_shapes=[pltpu.VMEM((SAMPLE_ROWS, 128), F32),
                        pltpu.VMEM((SB_HEADS * SAMPLE_ROWS, 128), F32),
                        pltpu.VMEM((n_state, 128), F32), pltpu.VMEM((n_state, 128), F32),
                        pltpu.VMEM((n_state, 128), F32)],
    )
    oa, oc = pl.pallas_call(
        kern, grid_spec=grid_spec,
        out_shape=[jax.ShapeDtypeStruct((nb, SB_HEADS * SAMPLE_ROWS, 128), F32),
                   jax.ShapeDtypeStruct((nb, DA_HEADS, SAMPLE_ROWS, 128), F32)],
        compiler_params=_params("arbitrary", "arbitrary"),
        name="sample_attn",
    )(page_table, qt, qs, new_page(1), new_page(2), dnkt, new_page(6), *cache_args,
      _tri_t_aug(), lp['da_lam_q1'], lp['da_lam_k1'], lp['da_lam_q2'], lp['da_lam_k2'],
      lp['da_norm_g'].reshape(1, DA_V_DIM))
    a_out = oa.reshape(nb, SB_HEADS, SAMPLE_ROWS, 128)[:, :, :ts].transpose(0, 2, 1, 3)
    c_out = oc[:, :, :ts].transpose(0, 2, 1, 3)
    return a_out.reshape(nb * ts, BRANCH_WIDTH), c_out.reshape(nb * ts, BRANCH_WIDTH)


LANE_GROUP = 512


def _s5_kernel(u_ref, bb_ref, a_ref, h0_ref, cc_ref, d_ref, glu_ref, y_ref, hf_ref,
               bu_s, st_s, *, nseq, steps, emit):
    i = pl.program_id(0)
    n = SSM_LANES

    @pl.when(i == 0)
    def _():
        st_s[...] = h0_ref[...]

    u = u_ref[...]
    bu_s[...] = jnp.dot(u.astype(BF16), bb_ref[...], preferred_element_type=F32)

    for sg in range(nseq // 8):
        for lg in range(n // LANE_GROUP):
            lre = slice(lg * LANE_GROUP, (lg + 1) * LANE_GROUP)
            lim = slice(n + lg * LANE_GROUP, n + (lg + 1) * LANE_GROUP)
            rs = slice(sg * 8, (sg + 1) * 8)
            ar = jnp.broadcast_to(a_ref[0:1, lre], (8, LANE_GROUP))
            ai = jnp.broadcast_to(a_ref[0:1, lim], (8, LANE_GROUP))

            def step(t, carry, lre=lre, lim=lim, sg=sg, ar=ar, ai=ai):
                hr, hi = carry
                r0 = pl.multiple_of(t * nseq + sg * 8, 8)
                nr = ar * hr - ai * hi + bu_s[pl.ds(r0, 8), lre]
                ni = ar * hi + ai * hr + bu_s[pl.ds(r0, 8), lim]
                if emit:
                    bu_s[pl.ds(r0, 8), lre] = nr
                    bu_s[pl.ds(r0, 8), lim] = ni
                return nr, ni

            hr, hi = lax.fori_loop(0, steps, step, (st_s[rs, lre], st_s[rs, lim]))
            st_s[rs, lre] = hr
            st_s[rs, lim] = hi

    if emit:
        y = jnp.dot(bu_s[...].astype(BF16), cc_ref[...], preferred_element_type=F32) + d_ref[...] * u
        y = _gelu_tanh(y)
        gate = jnp.dot(y.astype(BF16), glu_ref[...], preferred_element_type=F32)
        y_ref[...] = y * _sigmoid(gate)
    else:
        y_ref[...] = jnp.zeros(y_ref.shape, F32)

    hf_ref[...] = st_s[...]


def _s5_scan(u_rows, ssm, h0, nseq, steps, emit):
    rows = u_rows.shape[0]
    blk_rows = steps * nseq
    n_chunks = rows // blk_rows
    kern = functools.partial(_s5_kernel, nseq=nseq, steps=steps, emit=emit)
    const = lambda i: (0, 0)
    y_rows = blk_rows if emit else 8
    y, hf = pl.pallas_call(
        kern,
        grid=(n_chunks,),
        in_specs=[
            pl.BlockSpec((blk_rows, SSM_WIDTH), lambda i: (i, 0)),
            pl.BlockSpec((SSM_WIDTH, 2 * SSM_LANES), const),
            pl.BlockSpec((1, 2 * SSM_LANES), const),
            pl.BlockSpec((nseq, 2 * SSM_LANES), const),
            pl.BlockSpec((2 * SSM_LANES, SSM_WIDTH), const),
            pl.BlockSpec((1, SSM_WIDTH), const),
            pl.BlockSpec((SSM_WIDTH, SSM_WIDTH), const),
        ],
        out_specs=[pl.BlockSpec((y_rows, SSM_WIDTH), (lambda i: (i, 0)) if emit else const),
                   pl.BlockSpec((nseq, 2 * SSM_LANES), const)],
        out_shape=[jax.ShapeDtypeStruct((rows if emit else 8, SSM_WIDTH), F32),
                   jax.ShapeDtypeStruct((nseq, 2 * SSM_LANES), F32)],
        scratch_shapes=[pltpu.VMEM((blk_rows, 2 * SSM_LANES), F32),
                        pltpu.VMEM((nseq, 2 * SSM_LANES), F32)],
        compiler_params=_params("arbitrary"),
        name="s5_emit" if emit else "s5_state",
    )(u_rows, ssm['bb'], ssm['a'], h0, ssm['cc'], ssm['d'], ssm['glu'])
    return y, hf


def _segment_init_kernel(a_ref, f_ref, o_ref, *, seg_len, nb, nseg):
    n = SSM_LANES
    pr = a_ref[:, :n]
    pi = a_ref[:, n:]
    rr, ri = None, None
    e = seg_len
    while e:
        if e & 1:
            if rr is None:
                rr, ri = pr, pi
            else:
                rr, ri = rr * pr - ri * pi, rr * pi + ri * pr
        e >>= 1
        if e:
            pr, pi = pr * pr - pi * pi, 2.0 * pr * pi
    for b in range(nb):
        hr = jnp.zeros((1, n), F32)
        hi = jnp.zeros((1, n), F32)
        for k in range(nseg):
            r = b * nseg + k
            o_ref[r:r + 1, :n] = hr
            o_ref[r:r + 1, n:] = hi
            fr = f_ref[r:r + 1, :n]
            fi = f_ref[r:r + 1, n:]
            hr, hi = rr * hr - ri * hi + fr, rr * hi + ri * hr + fi


def _segment_init(a, f, seg_len, nb, nseg):
    kern = functools.partial(_segment_init_kernel, seg_len=seg_len, nb=nb, nseg=nseg)
    return pl.pallas_call(
        kern, out_shape=jax.ShapeDtypeStruct(f.shape, F32), name="s5_segments",
        compiler_params=pltpu.CompilerParams(vmem_limit_bytes=VMEM_LIMIT),
    )(a, f)


def _ssm_params(lp):
    lr = lp['ssm_lam_re']
    li = lp['ssm_lam_im']
    dt = jnp.exp(lp['ssm_log_dt'])[:, None]
    mag = jnp.exp(lr * dt)
    ab_re = mag * jnp.cos(li * dt)
    ab_im = mag * jnp.sin(li * dt)
    den = lr * lr + li * li
    f_re = ((ab_re - 1.0) * lr + ab_im * li) / den
    f_im = (ab_im * lr - (ab_re - 1.0) * li) / den
    b_re = lp['ssm_b_re']
    b_im = lp['ssm_b_im']
    bb_re = f_re[..., None] * b_re - f_im[..., None] * b_im
    bb_im = f_re[..., None] * b_im + f_im[..., None] * b_re
    eye = jnp.eye(SSM_GROUPS, dtype=F32)

    def in_blockdiag(w):
        return jnp.einsum('gnp,gh->gphn', w, eye).reshape(SSM_WIDTH, SSM_LANES)

    def out_blockdiag(w):
        return jnp.einsum('gpn,gh->gnhp', w, eye).reshape(SSM_LANES, SSM_WIDTH)

    return {
        'bb': jnp.concatenate([in_blockdiag(bb_re), in_blockdiag(bb_im)], axis=1).astype(BF16),
        'cc': jnp.concatenate([out_blockdiag(lp['ssm_c_re']), -out_blockdiag(lp['ssm_c_im'])],
                              axis=0).astype(BF16),
        'a': jnp.concatenate([ab_re.reshape(1, SSM_LANES), ab_im.reshape(1, SSM_LANES)], axis=1),
        'd': lp['ssm_d'].reshape(1, SSM_WIDTH),
        'glu': lp['ssm_w_glu'].astype(BF16),
    }


def _s5_prompt(u, ssm, bsz, t, nseg, steps):
    nseq = bsz * nseg
    seg_len = t // nseg
    u_rows = u.reshape(bsz, nseg, seg_len, SSM_WIDTH).transpose(2, 0, 1, 3).reshape(seg_len * nseq, SSM_WIDTH)
    zero = jnp.zeros((nseq, 2 * SSM_LANES), F32)
    _, f = _s5_scan(u_rows, ssm, zero, nseq, steps, emit=False)
    h0 = _segment_init(ssm['a'], f, seg_len, bsz, nseg)
    y_rows, hf = _s5_scan(u_rows, ssm, h0, nseq, steps, emit=True)
    y = y_rows.reshape(seg_len, bsz, nseg, SSM_WIDTH).transpose(1, 2, 0, 3).reshape(bsz * t, SSM_WIDTH)
    last = hf.reshape(bsz, nseg, 2 * SSM_LANES)[:, -1]
    return y, last[:, :SSM_LANES], last[:, SSM_LANES:]


def _s5_sample(u, ssm, h0_re, h0_im, nb, t):
    u_rows = u.reshape(nb, t, SSM_WIDTH).transpose(1, 0, 2).reshape(t * nb, SSM_WIDTH)
    h0 = jnp.concatenate([h0_re.reshape(nb, SSM_LANES), h0_im.reshape(nb, SSM_LANES)], axis=1)
    y_rows, hf = _s5_scan(u_rows, ssm, h0, nb, t, emit=True)
    y = y_rows.reshape(t, nb, SSM_WIDTH).transpose(1, 0, 2).reshape(nb * t, SSM_WIDTH)
    return y, hf[:, :SSM_LANES], hf[:, SSM_LANES:]


def _merge_kernel(x_ref, a_ref, b_ref, c_ref, wg0, wg1, wg2, wb_ref, o_ref):
    x = x_ref[...]
    acc = None
    for n, (br, wg) in enumerate(((a_ref, wg0), (b_ref, wg1), (c_ref, wg2))):
        gate = _sigmoid(jnp.dot(x, wg[...], preferred_element_type=F32))
        pb = jnp.dot(br[...].astype(BF16), wb_ref[n], preferred_element_type=F32)
        acc = gate * pb if acc is None else acc + gate * pb
    o_ref[...] = acc.astype(o_ref.dtype)


def _merge(x_bf, a_out, b_out, c_out, w_in_bf, w_branch_bf, tm, tn):
    m = x_bf.shape[0]
    ncol = D_MODEL // tn
    g0 = GATE_COL0 // tn

    def gate_spec(n):
        return pl.BlockSpec((D_MODEL, tn), lambda i, j: (0, g0 + n * ncol + j))

    br_spec = pl.BlockSpec((tm, BRANCH_WIDTH), lambda i, j: (i, 0))
    return pl.pallas_call(
        _merge_kernel,
        grid=(m // tm, ncol),
        in_specs=[pl.BlockSpec((tm, D_MODEL), lambda i, j: (i, 0)), br_spec, br_spec, br_spec,
                  gate_spec(0), gate_spec(1), gate_spec(2),
                  pl.BlockSpec((N_BRANCH, BRANCH_WIDTH, tn), lambda i, j: (0, 0, j))],
        out_specs=pl.BlockSpec((tm, tn), lambda i, j: (i, j)),
        out_shape=jax.ShapeDtypeStruct((m, D_MODEL), BF16),
        compiler_params=_params("arbitrary", "arbitrary"),
        name="merge",
    )(x_bf, a_out, b_out, c_out, w_in_bf, w_in_bf, w_in_bf, w_branch_bf)


def _layer_norm(v, g, b):
    mu = jnp.mean(v, axis=-1, keepdims=True)
    d = v - mu
    var = jnp.mean(d * d, axis=-1, keepdims=True)
    return d * lax.rsqrt(var + LN_EPS) * g + b


def _out_ln_kernel(m_ref, w_ref, x_ref, g_ref, b_ref, h_ref, hb_ref, *, alpha):
    mix = jnp.dot(m_ref[...], w_ref[...], preferred_element_type=F32)
    h = _layer_norm(alpha * x_ref[...] + mix, g_ref[...], b_ref[...])
    h_ref[...] = h
    hb_ref[...] = h.astype(BF16)


def _out_ln(merged_bf, w_out_bf, x, g, b, alpha, tm):
    m = x.shape[0]
    row = pl.BlockSpec((tm, D_MODEL), lambda i: (i, 0))
    vec = pl.BlockSpec((1, D_MODEL), lambda i: (0, 0))
    return pl.pallas_call(
        functools.partial(_out_ln_kernel, alpha=alpha),
        grid=(m // tm,),
        in_specs=[row, pl.BlockSpec((D_MODEL, D_MODEL), lambda i: (0, 0)), row, vec, vec],
        out_specs=[row, row],
        out_shape=[jax.ShapeDtypeStruct((m, D_MODEL), F32), jax.ShapeDtypeStruct((m, D_MODEL), BF16)],
        compiler_params=_params("arbitrary"),
        name="out_ln",
    )(merged_bf, w_out_bf, x, g.reshape(1, D_MODEL), b.reshape(1, D_MODEL))


def _staircase_pairs(k):
    return [(a, b) for a in range(k) for b in range(k) if (a + 1) * (b + 1) <= k]


def _kth_largest(vals, k):
    neg = -jnp.inf
    cur = functools.reduce(jnp.maximum, vals)
    for _ in range(k - 1):
        nxt = None
        for v in vals:
            c = jnp.where(v < cur, v, neg)
            nxt = c if nxt is None else jnp.maximum(nxt, c)
        cur = nxt
    return cur


def _peer_route_kernel(h_ref, wq_ref, keys_ref, t0_ref, s1_ref, e0_ref, e1_ref,
                       q_s, s0_s, top_s, *, tm):
    q_s[...] = jnp.dot(h_ref[...], wq_ref[...], preferred_element_type=F32).astype(BF16)
    neg = -jnp.inf
    for hd in range(PEER_HEADS):
        for c in range(2):
            col = (hd * 2 + c) * PEER_HALF
            s = lax.dot_general(keys_ref[hd, c], q_s[:, col:col + PEER_HALF], NT_DIMS,
                                preferred_element_type=F32)
            (s0_s if c == 0 else s1_ref)[hd] = s
            cur = jnp.max(s, axis=0, keepdims=True)
            top_s[c, 0, hd:hd + 1, :] = cur
            for a in range(1, PEER_TOPK):
                cur = jnp.max(jnp.where(s < cur, s, neg), axis=0, keepdims=True)
                top_s[c, a, hd:hd + 1, :] = cur
    pairs = _staircase_pairs(PEER_TOPK)
    sums = [top_s[0, a] + top_s[1, b] for a, b in pairs]
    tau = _kth_largest(sums, PEER_TOPK)
    best = top_s[0, 0] + top_s[1, 0]
    z = None
    partner = [None] * PEER_TOPK
    for (a, b), v in zip(pairs, sums):
        sel = v >= tau
        e = jnp.where(sel, jnp.exp(v - best), 0.0)
        z = e if z is None else z + e
        cand = jnp.where(sel, top_s[1, b], jnp.inf)
        partner[a] = cand if partner[a] is None else jnp.minimum(partner[a], cand)
    inv_z = 1.0 / z
    for hd in range(PEER_HEADS):
        row = slice(hd, hd + 1)
        s0 = s0_s[hd]
        t0 = jnp.full(s0.shape, jnp.inf, F32)
        for a in range(PEER_TOPK):
            t0 = jnp.where(s0 == top_s[0, a, row, :], partner[a][row, :], t0)
        t0_ref[hd] = t0
        e0_ref[hd] = jnp.exp(s0 - top_s[0, 0, row, :]) * inv_z[row, :]
        e1_ref[hd] = jnp.exp(s1_ref[hd] - top_s[1, 0, row, :])


def _peer_route(h_bf, wq_bf, keys_bf, tm):
    m = h_bf.shape[0]
    tab = pl.BlockSpec((PEER_HEADS, PEER_KEYS, tm), lambda i: (0, 0, i))
    tab_shape = jax.ShapeDtypeStruct((PEER_HEADS, PEER_KEYS, m), F32)
    return pl.pallas_call(
        functools.partial(_peer_route_kernel, tm=tm),
        grid=(m // tm,),
        in_specs=[pl.BlockSpec((tm, D_MODEL), lambda i: (i, 0)),
                  pl.BlockSpec((D_MODEL, D_MODEL), lambda i: (0, 0)),
                  pl.BlockSpec((PEER_HEADS, 2, PEER_KEYS, PEER_HALF), lambda i: (0, 0, 0, 0))],
        out_specs=[tab, tab, tab, tab],
        out_shape=[tab_shape, tab_shape, tab_shape, tab_shape],
        scratch_shapes=[pltpu.VMEM((tm, D_MODEL), BF16),
                        pltpu.VMEM((PEER_HEADS, PEER_KEYS, tm), F32),
                        pltpu.VMEM((2, PEER_TOPK, PEER_HEADS, tm), F32)],
        compiler_params=_params("arbitrary"),
        name="peer_route",
    )(h_bf, wq_bf, keys_bf)


GATE_ROWS = 16


def _peer_gates(ht_ref, p_ref, t0_ref, s1_ref, e0_ref, e1_ref, blk, eb):
    for ii in range(eb // PEER_KEYS):
        i = blk * (eb // PEER_KEYS) + ii
        rows_i = [(t0_ref[hd, pl.ds(i, 1), :], e0_ref[hd, pl.ds(i, 1), :]) for hd in range(PEER_HEADS)]
        for rt in range(PEER_KEYS // GATE_ROWS):
            r = slice(rt * GATE_ROWS, (rt + 1) * GATE_ROWS)
            w = None
            for hd in range(PEER_HEADS):
                t0, e0 = rows_i[hd]
                sel = jnp.where(s1_ref[hd, r, :] >= t0, e0 * e1_ref[hd, r, :], 0.0)
                w = sel if w is None else w + sel
            ro = slice(ii * PEER_KEYS + rt * GATE_ROWS, ii * PEER_KEYS + (rt + 1) * GATE_ROWS)
            p_ref[ro, :] = (_gelu_tanh(ht_ref[ro, :]) * w).astype(BF16)


def _peer_expert_kernel(h_ref, ua_ref, ub_ref, vta_ref, vtb_ref, t0_ref, s1_ref, e0_ref, e1_ref,
                        o_ref, ht_0, ht_1, p_0, p_1, *, eb, n_eb):
    g = pl.program_id(1)
    last = pl.num_programs(1) - 1
    tabs = (t0_ref, s1_ref, e0_ref, e1_ref)

    def first_matmul(u_ref, ht_ref):
        ht_ref[...] = lax.dot_general(u_ref[...], h_ref[...], NT_DIMS, preferred_element_type=F32)

    def second_matmul(vt_ref, p_ref):
        o_ref[...] += jnp.dot(vt_ref[...], p_ref[...], preferred_element_type=F32)

    @pl.when(g == 0)
    def _():
        o_ref[...] = jnp.zeros(o_ref.shape, F32)
        ht_1[...] = jnp.zeros(ht_1.shape, F32)
        p_0[...] = jnp.zeros(p_0.shape, BF16)

    @pl.when(g < last)
    def _():
        first_matmul(ua_ref, ht_0)
        _peer_gates(ht_1, p_1, *tabs, jnp.maximum(2 * g - 1, 0), eb)
        second_matmul(vta_ref, p_0)

    @pl.when(g + 1 <= last)
    def _():
        first_matmul(ub_ref, ht_1)
        _peer_gates(ht_0, p_0, *tabs, 2 * g, eb)
        second_matmul(vtb_ref, p_1)

    @pl.when(g == last)
    def _():
        _peer_gates(ht_1, p_1, *tabs, n_eb - 1, eb)
        second_matmul(vta_ref, p_0)
        second_matmul(vtb_ref, p_1)


def _peer_experts(h_bf, u_bf, vt_bf, route, tb, eb):
    m = h_bf.shape[0]
    n_eb = u_bf.shape[0] // eb
    assert n_eb % 2 == 0
    hi = n_eb - 1
    tab = pl.BlockSpec((PEER_HEADS, PEER_KEYS, tb), lambda t, g: (0, 0, t))
    return pl.pallas_call(
        functools.partial(_peer_expert_kernel, eb=eb, n_eb=n_eb),
        grid=(m // tb, n_eb // 2 + 1),
        in_specs=[pl.BlockSpec((tb, D_MODEL), lambda t, g: (t, 0)),
                  pl.BlockSpec((eb, D_MODEL), lambda t, g: (jnp.minimum(2 * g, hi), 0)),
                  pl.BlockSpec((eb, D_MODEL), lambda t, g: (jnp.minimum(2 * g + 1, hi), 0)),
                  pl.BlockSpec((D_MODEL, eb), lambda t, g: (0, jnp.clip(2 * g - 2, 0, hi))),
                  pl.BlockSpec((D_MODEL, eb), lambda t, g: (0, jnp.clip(2 * g - 1, 0, hi))),
                  tab, tab, tab, tab],
        out_specs=pl.BlockSpec((D_MODEL, tb), lambda t, g: (0, t)),
        out_shape=jax.ShapeDtypeStruct((D_MODEL, m), F32),
        scratch_shapes=[pltpu.VMEM((eb, tb), F32), pltpu.VMEM((eb, tb), F32),
                        pltpu.VMEM((eb, tb), BF16), pltpu.VMEM((eb, tb), BF16)],
        compiler_params=_params("arbitrary", "arbitrary"),
        name="peer_experts",
    )(h_bf, u_bf, u_bf, vt_bf, vt_bf, *route)


def _ffn_ln_kernel(ft_ref, h_ref, g_ref, b_ref, y_ref, yb_ref, *, alpha):
    y = _layer_norm(alpha * h_ref[...] + ft_ref[...].T, g_ref[...], b_ref[...])
    y_ref[...] = y
    yb_ref[...] = y.astype(BF16)


def _ffn_ln(ffn_t, h, g, b, alpha, tm):
    m = h.shape[0]
    row = pl.BlockSpec((tm, D_MODEL), lambda i: (i, 0))
    vec = pl.BlockSpec((1, D_MODEL), lambda i: (0, 0))
    return pl.pallas_call(
        functools.partial(_ffn_ln_kernel, alpha=alpha),
        grid=(m // tm,),
        in_specs=[pl.BlockSpec((D_MODEL, tm), lambda i: (0, i)), row, vec, vec],
        out_specs=[row, row],
        out_shape=[jax.ShapeDtypeStruct((m, D_MODEL), F32), jax.ShapeDtypeStruct((m, D_MODEL), BF16)],
        compiler_params=_params("arbitrary"),
        name="ffn_ln",
    )(ffn_t, h, g.reshape(1, D_MODEL), b.reshape(1, D_MODEL))


def _row_tile(m, pref):
    t = min(m, pref)
    assert m % t == 0, (m, t)
    return t


def _token_mix_tail(x, x_bf, a_out, b_out, c_out, w, lp, alpha):
    m = x.shape[0]
    merged = _merge(x_bf, a_out, b_out, c_out, w['w_in'], w['w_branch'],
                    _row_tile(m, 512), 512)
    h, h_bf = _out_ln(merged, w['w_out'], x, lp['ln1_g'], lp['ln1_b'], alpha, _row_tile(m, 256))
    route = _peer_route(h_bf, w['peer_w_q'], w['peer_keys'], _row_tile(m, 256))
    ffn_t = _peer_experts(h_bf, w['peer_u'], w['peer_vt'], route, _row_tile(m, 512), 512)
    return _ffn_ln(ffn_t, h, lp['ln2_g'], lp['ln2_b'], alpha, _row_tile(m, 256))


def _prompt_layer(x, x_bf, lp, w, ssm, rope_tabs, bsz, t, lambda_init, alpha):
    proj = _project(x_bf, w['w_in'], rope_tabs, _row_tile(bsz * t, 512))
    a_out = _sb_prompt(proj, bsz, t, _row_tile(t, 256))
    c_out = _da_prompt(proj, lp, bsz, t, _row_tile(t, 512), lambda_init)
    nseg = 8 // bsz if 8 % bsz == 0 and t % (8 // bsz) == 0 else 1
    assert (bsz * nseg) % 8 == 0
    seg_len = t // nseg
    b_out, h_re, h_im = _s5_prompt(proj[3], ssm, bsz, t, nseg, _row_tile(seg_len, 64))
    y, y_bf = _token_mix_tail(x, x_bf, a_out, b_out, c_out, w, lp, alpha)
    state = (proj[1], proj[2], proj[5], proj[6], h_re, h_im)
    return y, y_bf, state


def _sample_layer(x, x_bf, lp, w, ssm, rope_tabs, caches, layer, page_table, h0_re, h0_im,
                  nb, t, lambda_init, alpha):
    proj = _project(x_bf, w['w_in'], rope_tabs, nb * t)
    a_out, c_out = _sample_attn(proj, caches, layer, page_table, lp, nb, t, lambda_init)
    b_out, h_re, h_im = _s5_sample(proj[3], ssm, h0_re, h0_im, nb, t)
    y, y_bf = _token_mix_tail(x, x_bf, a_out, b_out, c_out, w, lp, alpha)
    state = (proj[1], proj[2], proj[5], proj[6], h_re, h_im)
    return y, y_bf, state


def kernel(x_prompt, x_sample, cache_sb_k, cache_sb_v, cache_da_k, cache_da_v, state_ssm_re, state_ssm_im, page_table, w_in, w_branch, w_out, ssm_lam_re, ssm_lam_im, ssm_log_dt, ssm_b_re, ssm_b_im, ssm_c_re, ssm_c_im, ssm_d, ssm_w_glu, da_lam_q1, da_lam_k1, da_lam_q2, da_lam_k2, da_norm_g, ln1_g, ln1_b, peer_w_q, peer_keys, peer_u, peer_v, ln2_g, ln2_b):
    depth = w_in.shape[0]
    bsz, t, _ = x_prompt.shape
    nb, ts, _ = x_sample.shape
    n_past = page_table.shape[1] * PAGE_SIZE
    alpha = (2 * depth) ** 0.25
    n_pool = cache_sb_k.shape[1]
    caches = (cache_sb_k.reshape(depth, n_pool, PAGE_ROWS, 128),
              cache_sb_v.reshape(depth, n_pool, PAGE_ROWS, 128),
              jnp.transpose(cache_da_k, (0, 1, 3, 4, 5, 2)).reshape(depth, n_pool, PAGE_ROWS, PAGE_SIZE),
              cache_da_v.reshape(depth, n_pool, PAGE_ROWS, 128))
    rope_p = _rope_tables(jnp.tile(jnp.arange(t, dtype=jnp.int32), bsz))
    rope_s = _rope_tables(jnp.tile(n_past + jnp.arange(ts, dtype=jnp.int32), nb))

    yp = x_prompt.reshape(bsz * t, D_MODEL)
    ys = x_sample.reshape(nb * ts, D_MODEL)
    yp_bf = yp.astype(BF16)
    ys_bf = ys.astype(BF16)
    rows_p, rows_s = [], []
    for l in range(depth):
        lp = {'ssm_lam_re': ssm_lam_re[l], 'ssm_lam_im': ssm_lam_im[l], 'ssm_log_dt': ssm_log_dt[l],
              'ssm_b_re': ssm_b_re[l], 'ssm_b_im': ssm_b_im[l],
              'ssm_c_re': ssm_c_re[l], 'ssm_c_im': ssm_c_im[l],
              'ssm_d': ssm_d[l], 'ssm_w_glu': ssm_w_glu[l],
              'da_lam_q1': da_lam_q1[l], 'da_lam_k1': da_lam_k1[l],
              'da_lam_q2': da_lam_q2[l], 'da_lam_k2': da_lam_k2[l], 'da_norm_g': da_norm_g[l],
              'ln1_g': ln1_g[l], 'ln1_b': ln1_b[l], 'ln2_g': ln2_g[l], 'ln2_b': ln2_b[l]}
        w = {'w_in': w_in[l].astype(BF16), 'w_branch': w_branch[l].astype(BF16),
             'w_out': w_out[l].astype(BF16), 'peer_w_q': peer_w_q[l].astype(BF16),
             'peer_keys': peer_keys[l].astype(BF16), 'peer_u': peer_u[l].astype(BF16),
             'peer_vt': peer_v[l].astype(BF16).T}
        ssm = _ssm_params(lp)
        lambda_init = 0.8 - 0.6 * math.exp(-0.3 * l)
        yp, yp_bf, st_p = _prompt_layer(yp, yp_bf, lp, w, ssm, rope_p, bsz, t, lambda_init, alpha)
        ys, ys_bf, st_s = _sample_layer(ys, ys_bf, lp, w, ssm, rope_s, caches, l, page_table,
                                        state_ssm_re[l], state_ssm_im[l], nb, ts, lambda_init, alpha)
        rows_p.append(st_p)
        rows_s.append(st_s)

    def pack(rows, lead):
        sk, sv, dk, dv, hr, hi = [jnp.stack(z) for z in zip(*rows)]
        n = lead[0]
        return (sk.reshape(depth, *lead, SB_HEADS, SB_HEAD_DIM),
                sv.reshape(depth, *lead, SB_HEADS, SB_HEAD_DIM),
                dk.reshape(depth, *lead, DA_HEADS, 2, DA_SUB_DIM),
                dv.reshape(depth, *lead, DA_HEADS, DA_V_DIM),
                hr.reshape(depth, n, SSM_GROUPS, SSM_STATE),
                hi.reshape(depth, n, SSM_GROUPS, SSM_STATE))

    return ((yp.reshape(bsz, t, D_MODEL), ys.reshape(nb, ts, D_MODEL))
            + pack(rows_p, (bsz, t)) + pack(rows_s, (nb, ts)))
```

```python
import functools
import math

import jax
import jax.numpy as jnp
from jax import lax
from jax.experimental import pallas as pl
from jax.experimental.pallas import tpu as pltpu

F32 = jnp.float32
BF16 = jnp.bfloat16

D_MODEL = 2048
PAGE_SIZE = 128
SB_HEADS = 4
SB_HEAD_DIM = 128
SSM_GROUP = 16
SSM_GROUPS = 32
SSM_STATE = 64
SSM_WIDTH = SSM_GROUPS * SSM_GROUP
SSM_LANES = SSM_GROUPS * SSM_STATE
DA_HEADS = 4
DA_SUB_DIM = 64
DA_V_DIM = 128
ROPE_THETA = 500000.0
ROPE_DIM = DA_SUB_DIM // 4
N_BRANCH = 3
BRANCH_WIDTH = 512
N_PROJ_GROUPS = 7
GATE_COL0 = N_PROJ_GROUPS * BRANCH_WIDTH
PEER_HEADS = 8
PEER_KEYS = 128
PEER_HALF = 128
PEER_TOPK = 16
LN_EPS = 1e-5

V7X_VMEM_BYTES = 64 * 1024 * 1024
VMEM_LIMIT = V7X_VMEM_BYTES - 8 * 1024 * 1024

NT_DIMS = (((1,), (1,)), ((), ()))

F32_EXP_UNDERFLOW = -105.0


def _params(*sem):
    return pltpu.CompilerParams(dimension_semantics=sem, vmem_limit_bytes=VMEM_LIMIT)


def _gelu_tanh(x):
    return 0.5 * x * (1.0 + jnp.tanh(math.sqrt(2.0 / math.pi) * (x + 0.044715 * (x * x * x))))


def _sigmoid(x):
    return 1.0 / (1.0 + jnp.exp(-x))


def _proj_kernel(x_ref, w_ref, cos_ref, sa_ref, sb_ref, o_ref):
    j = pl.program_id(0)
    o_ref[0] = jnp.dot(x_ref[...], w_ref[...], preferred_element_type=F32)

    @pl.when((j == 4) | (j == 5))
    def _():
        c = cos_ref[...]
        sa = sa_ref[...]
        sb = sb_ref[...]
        for t in range(BRANCH_WIDTH // 128):
            xt = o_ref[0, :, t * 128:(t + 1) * 128]
            o_ref[0, :, t * 128:(t + 1) * 128] = (
                xt * c + pltpu.roll(xt, 8, 1) * sa + pltpu.roll(xt, 120, 1) * sb)


def _project(x_bf, w_in_bf, rope_tabs, tm):
    m = x_bf.shape[0]
    nr = m // tm
    cos_t, sa_t, sb_t = rope_tabs

    def tab_map(j, i):
        return (jnp.where((j == 4) | (j == 5), i, 0), 0)

    return pl.pallas_call(
        _proj_kernel,
        grid=(N_PROJ_GROUPS, nr),
        in_specs=[
            pl.BlockSpec((tm, D_MODEL), lambda j, i: (i, 0)),
            pl.BlockSpec((D_MODEL, BRANCH_WIDTH), lambda j, i: (0, j)),
            pl.BlockSpec((tm, 128), tab_map),
            pl.BlockSpec((tm, 128), tab_map),
            pl.BlockSpec((tm, 128), tab_map),
        ],
        out_specs=pl.BlockSpec((1, tm, BRANCH_WIDTH), lambda j, i: (j, i, 0)),
        out_shape=jax.ShapeDtypeStruct((N_PROJ_GROUPS, m, BRANCH_WIDTH), F32),
        compiler_params=_params("arbitrary", "arbitrary"),
        name="proj",
    )(x_bf, w_in_bf, cos_t, sa_t, sb_t)


def _rope_tables(pos):
    half = ROPE_DIM // 2
    inv_freq = ROPE_THETA ** (-jnp.arange(half, dtype=F32) / half)
    ang = pos.astype(F32)[:, None] * inv_freq[None, :]
    cos = jnp.cos(ang)
    sin = jnp.sin(ang)
    n = pos.shape[0]
    ones = jnp.ones((n, DA_SUB_DIM - ROPE_DIM), F32)
    zeros = jnp.zeros((n, DA_SUB_DIM - ROPE_DIM), F32)
    zh = jnp.zeros((n, half), F32)
    c64 = jnp.concatenate([cos, cos, ones], axis=1)
    sa64 = jnp.concatenate([zh, sin, zeros], axis=1)
    sb64 = jnp.concatenate([-sin, zh, zeros], axis=1)
    return tuple(jnp.concatenate([t, t], axis=1) for t in (c64, sa64, sb64))


def _log_sigmoid(z):
    return jnp.minimum(z, 0.0) - jnp.log1p(jnp.exp(-jnp.abs(z)))


def _split_bf16(x):
    hi = x.astype(BF16)
    lo = (x - hi.astype(F32)).astype(BF16)
    return hi, lo


def _sb_block(q, kb, vb, tri, c, acc, scale, mask):
    tk = kb.shape[0]
    z = lax.dot_general(q, kb, NT_DIMS, preferred_element_type=F32) * scale
    lb = _log_sigmoid(z)
    lk = lb - z
    if mask is not None:
        lk = jnp.where(mask, lk, 0.0)
    hi, lo = _split_bf16(lk)
    r = (jnp.dot(hi, tri, preferred_element_type=F32)
         + jnp.dot(lo, tri, preferred_element_type=F32))
    w = jnp.exp(lb + r[:, :tk] + c)
    if mask is not None:
        w = jnp.where(mask, w, 0.0)
    acc = acc + jnp.dot(w.astype(BF16), vb, preferred_element_type=F32)
    return c + r[:, tk:], acc


def _sb_prompt_kernel(q_ref, k_ref, v_ref, tri_ref, o_ref, kbf, vbf, *, blk, scale):
    qi = pl.program_id(2)

    @pl.when(qi == 0)
    def _():
        kbf[...] = k_ref[0].astype(BF16)
        vbf[...] = v_ref[0].astype(BF16)

    q = q_ref[0].astype(BF16)
    tri = tri_ref[...]
    row = lax.broadcasted_iota(jnp.int32, (blk, blk), 0)
    col = lax.broadcasted_iota(jnp.int32, (blk, blk), 1)
    off = pl.multiple_of(qi * blk, blk)
    c0 = jnp.zeros((blk, blk), F32)
    a0 = jnp.zeros((blk, SB_HEAD_DIM), F32)
    c, acc = _sb_block(q, kbf[pl.ds(off, blk), :], vbf[pl.ds(off, blk), :], tri, c0, a0,
                       scale, col < row)

    def cond(carry):
        jj, _, _, cmax = carry
        return (jj < qi) & (cmax > F32_EXP_UNDERFLOW)

    def body(carry):
        jj, c, acc, _ = carry
        o = pl.multiple_of((qi - 1 - jj) * blk, blk)
        c, acc = _sb_block(q, kbf[pl.ds(o, blk), :], vbf[pl.ds(o, blk), :], tri, c, acc, scale, None)
        return jj + 1, c, acc, jnp.max(c[:, :128])

    _, _, acc, _ = lax.while_loop(cond, body, (jnp.int32(0), c, acc, jnp.max(c[:, :128])))
    o_ref[...] = acc


def _tri_ones(blk):
    j = jnp.arange(blk)[:, None]
    s = jnp.arange(blk)[None, :]
    return jnp.concatenate([(j > s).astype(BF16), jnp.ones((blk, blk), BF16)], axis=1)


def _sb_prompt(proj, bsz, t, blk):
    nq = t // blk
    kern = functools.partial(_sb_prompt_kernel, blk=blk, scale=SB_HEAD_DIM ** -0.5)
    return pl.pallas_call(
        kern,
        grid=(bsz, SB_HEADS, nq),
        in_specs=[
            pl.BlockSpec((1, blk, SB_HEAD_DIM), lambda b, h, i: (0, b * nq + i, h)),
            pl.BlockSpec((1, t, SB_HEAD_DIM), lambda b, h, i: (1, b, h)),
            pl.BlockSpec((1, t, SB_HEAD_DIM), lambda b, h, i: (2, b, h)),
            pl.BlockSpec((blk, 2 * blk), lambda b, h, i: (0, 0)),
        ],
        out_specs=pl.BlockSpec((blk, SB_HEAD_DIM), lambda b, h, i: (b * nq + i, h)),
        out_shape=jax.ShapeDtypeStruct((bsz * t, SB_HEADS * SB_HEAD_DIM), F32),
        scratch_shapes=[pltpu.VMEM((t, SB_HEAD_DIM), BF16), pltpu.VMEM((t, SB_HEAD_DIM), BF16)],
        compiler_params=_params("arbitrary", "arbitrary", "arbitrary"),
        name="sb_prompt",
    )(proj, proj, proj, _tri_ones(blk))


def _da_lambda(lq1, lk1, lq2, lk2, h, lambda_init):
    a = jnp.sum(lq1[pl.ds(h, 1), :] * lk1[pl.ds(h, 1), :], axis=1, keepdims=True)
    b = jnp.sum(lq2[pl.ds(h, 1), :] * lk2[pl.ds(h, 1), :], axis=1, keepdims=True)
    return jnp.exp(a) - jnp.exp(b) + lambda_init


def _split_components(q):
    lane = lax.broadcasted_iota(jnp.int32, q.shape, 1)
    q0 = jnp.where(lane < DA_SUB_DIM, q, 0.0).astype(BF16)
    q1 = jnp.where(lane >= DA_SUB_DIM, q, 0.0).astype(BF16)
    return q0, q1


def _softmax_block(qc, kb, vb, m, l, acc, scale, mask):
    s = lax.dot_general(qc, kb, NT_DIMS, preferred_element_type=F32) * scale
    yield
    if mask is not None:
        s = jnp.where(mask, s, -jnp.inf)
    m_new = jnp.maximum(m, jnp.max(s, axis=1, keepdims=True))
    p = jnp.exp(s - m_new)
    alpha = jnp.exp(m - m_new)
    l = alpha * l + jnp.sum(p, axis=1, keepdims=True)
    p = p.astype(BF16)
    yield
    acc = alpha * acc + jnp.dot(p, vb, preferred_element_type=F32)
    return m_new, l, acc


def _da_finish(st0, st1, lam, g, lambda_init):
    o = st0[2] / st0[1] - lam * (st1[2] / st1[1])
    o = o * lax.rsqrt(jnp.mean(o * o, axis=-1, keepdims=True) + LN_EPS) * g
    return (1.0 - lambda_init) * o


def _da_prompt_kernel(q_ref, k_ref, v_ref, lq1, lk1, lq2, lk2, g_ref, o_ref, kbf, vbf,
                      *, blk, scale, lambda_init):
    h = pl.program_id(1)
    qi = pl.program_id(2)

    @pl.when(qi == 0)
    def _():
        kbf[...] = k_ref[0].astype(BF16)
        vbf[...] = v_ref[0].astype(BF16)

    q0, q1 = _split_components(q_ref[0])
    row = lax.broadcasted_iota(jnp.int32, (blk, blk), 0)
    col = lax.broadcasted_iota(jnp.int32, (blk, blk), 1)
    mask = col <= row
    off = pl.multiple_of(qi * blk, blk)
    init = (jnp.full((blk, 1), -jnp.inf, F32), jnp.zeros((blk, 1), F32),
            jnp.zeros((blk, DA_V_DIM), F32))
    kd = kbf[pl.ds(off, blk), :]
    vd = vbf[pl.ds(off, blk), :]
    st0, st1 = _run_staged(_softmax_block(q0, kd, vd, *init, scale, mask),
                           _softmax_block(q1, kd, vd, *init, scale, mask))

    def body(j, carry):
        o = pl.multiple_of(j * blk, blk)
        kb = kbf[pl.ds(o, blk), :]
        vb = vbf[pl.ds(o, blk), :]
        return tuple(_run_staged(_softmax_block(q0, kb, vb, *carry[0], scale, None),
                                 _softmax_block(q1, kb, vb, *carry[1], scale, None)))

    st0, st1 = lax.fori_loop(0, qi, body, (st0, st1))
    lam = _da_lambda(lq1, lk1, lq2, lk2, h, lambda_init)
    o_ref[...] = _da_finish(st0, st1, lam, g_ref[...], lambda_init)


def _da_prompt(proj, lp, bsz, t, blk, lambda_init):
    nq = t // blk
    kern = functools.partial(_da_prompt_kernel, blk=blk, scale=DA_SUB_DIM ** -0.5,
                             lambda_init=lambda_init)
    lam_spec = pl.BlockSpec((DA_HEADS, DA_SUB_DIM), lambda b, h, i: (0, 0))
    return pl.pallas_call(
        kern,
        grid=(bsz, DA_HEADS, nq),
        in_specs=[
            pl.BlockSpec((1, blk, 128), lambda b, h, i: (4, b * nq + i, h)),
            pl.BlockSpec((1, t, 128), lambda b, h, i: (5, b, h)),
            pl.BlockSpec((1, t, DA_V_DIM), lambda b, h, i: (6, b, h)),
            lam_spec, lam_spec, lam_spec, lam_spec,
            pl.BlockSpec((1, DA_V_DIM), lambda b, h, i: (0, 0)),
        ],
        out_specs=pl.BlockSpec((blk, DA_V_DIM), lambda b, h, i: (b * nq + i, h)),
        out_shape=jax.ShapeDtypeStruct((bsz * t, DA_HEADS * DA_V_DIM), F32),
        scratch_shapes=[pltpu.VMEM((t, 128), BF16), pltpu.VMEM((t, DA_V_DIM), BF16)],
        compiler_params=_params("arbitrary", "arbitrary", "arbitrary"),
        name="da_prompt",
    )(proj, proj, proj, lp['da_lam_q1'], lp['da_lam_k1'], lp['da_lam_q2'], lp['da_lam_k2'],
      lp['da_norm_g'].reshape(1, DA_V_DIM))


SAMPLE_ROWS = 8
PAGES_PER_STEP = 8
PAGE_ROWS = PAGE_SIZE * SB_HEADS


def _tri_t_aug():
    s = jnp.arange(PAGE_SIZE + 8)[:, None]
    j = jnp.arange(PAGE_SIZE)[None, :]
    return ((j > s) | (s >= PAGE_SIZE)).astype(BF16)


def _head_rows(refs, h, lead):
    idx = (pl.ds(h, PAGE_SIZE, stride=SB_HEADS), slice(None))
    parts = [(ref[(0,) + idx] if lead else ref[idx]).astype(BF16) for ref in refs]
    return parts[0] if len(parts) == 1 else jnp.concatenate(parts, axis=0)


def _run_staged(*gens):
    results = [None] * len(gens)
    live = list(range(len(gens)))
    while live:
        for idx in list(live):
            try:
                next(gens[idx])
            except StopIteration as stop:
                results[idx] = stop.value
                live.remove(idx)
    return results


def _sb_pages(qt_ref, k_refs, v_refs, lead, tri_t, c, acc, mask):
    z = None
    for h in range(SB_HEADS):
        d = jnp.dot(_head_rows(k_refs, h, lead), qt_ref[0, h * 128:(h + 1) * 128, :],
                    preferred_element_type=F32)
        z = d if z is None else z + d
    yield
    z = z * (SB_HEAD_DIM ** -0.5)
    lb = _log_sigmoid(z)
    lk = lb - z
    if mask is not None:
        lk = jnp.where(mask, lk, 0.0)
    hi, lo = _split_bf16(lk)
    yield
    later = []
    for k in range(len(k_refs)):
        rows = slice(k * PAGE_SIZE, (k + 1) * PAGE_SIZE)
        r = (jnp.dot(tri_t, hi[rows], preferred_element_type=F32)
             + jnp.dot(tri_t, lo[rows], preferred_element_type=F32))
        later.append(r[:PAGE_SIZE] + c[0:1, :])
        c = c + r[PAGE_SIZE:]
    yield
    later = later[0] if len(later) == 1 else jnp.concatenate(later, axis=0)
    w = jnp.exp(lb + later)
    if mask is not None:
        w = jnp.where(mask, w, 0.0)
    wt = w.T
    yield
    parts = []
    for h in range(SB_HEADS):
        rows = slice(h * SAMPLE_ROWS, (h + 1) * SAMPLE_ROWS)
        parts.append(acc[rows] + jnp.dot(wt[rows].astype(BF16), _head_rows(v_refs, h, lead),
                                         preferred_element_type=F32))
    return c, jnp.concatenate(parts, axis=0)


def _da_pages(qs_ref, kt_refs, v_refs, lead, m, l, acc, mask):
    scale = DA_SUB_DIM ** -0.5
    nr = 2 * SAMPLE_ROWS
    scores = []
    for h in range(DA_HEADS):
        hs = slice(h * 128, (h + 1) * 128)
        kts = [(ref[0, hs, :] if lead else ref[hs, :]).astype(BF16) for ref in kt_refs]
        kt = kts[0] if len(kts) == 1 else jnp.concatenate(kts, axis=1)
        scores.append(jnp.dot(qs_ref[0, h], kt, preferred_element_type=F32) * scale)
    yield
    ms, ls, ps, alphas = [], [], [], []
    for h in range(DA_HEADS):
        rows = slice(h * nr, (h + 1) * nr)
        s = scores[h]
        if mask is not None:
            s = jnp.where(mask, s, -jnp.inf)
        m_old = m[rows]
        m_new = jnp.maximum(m_old, jnp.max(s, axis=1, keepdims=True))
        p = jnp.exp(s - m_new[:, :1])
        alphas.append(jnp.exp(m_old - m_new))
        ms.append(m_new)
        ls.append(alphas[h] * l[rows] + jnp.sum(p, axis=1, keepdims=True))
        ps.append(p.astype(BF16))
    yield
    accs = []
    for h in range(DA_HEADS):
        rows = slice(h * nr, (h + 1) * nr)
        accs.append(alphas[h] * acc[rows] + jnp.dot(ps[h], _head_rows(v_refs, h, lead),
                                                    preferred_element_type=F32))
    return (jnp.concatenate(ms, axis=0), jnp.concatenate(ls, axis=0), jnp.concatenate(accs, axis=0))


def _sample_attn_kernel(pt_ref, qt_ref, qs_ref, nk_ref, nv_ref, dnkt_ref, dnv_ref, *rest,
                        n_new, lambda_init):
    del pt_ref
    npp = PAGES_PER_STEP
    cache_refs = rest[:4 * npp]
    tri_ref, lq1, lk1, lq2, lk2, g_ref, oa_ref, oc_ref, c_s, acc_s, m_s, l_s, dacc_s = rest[4 * npp:]
    s = pl.program_id(1)
    tri_t = tri_ref[...]
    nr = 2 * SAMPLE_ROWS

    @pl.when(s == 0)
    def _():
        key = lax.broadcasted_iota(jnp.int32, (PAGE_SIZE, 128), 0)
        qry = lax.broadcasted_iota(jnp.int32, (PAGE_SIZE, 128), 1) % SAMPLE_ROWS
        sb_mask = (key < n_new) & (key < qry)
        qry = lax.broadcasted_iota(jnp.int32, (nr, PAGE_SIZE), 0) % SAMPLE_ROWS
        key = lax.broadcasted_iota(jnp.int32, (nr, PAGE_SIZE), 1)
        da_mask = (key < n_new) & (key <= qry)
        (c, acc), (m, l, dacc) = _run_staged(
            _sb_pages(qt_ref, [nk_ref], [nv_ref], True, tri_t,
                      jnp.zeros((SAMPLE_ROWS, 128), F32),
                      jnp.zeros((SB_HEADS * SAMPLE_ROWS, 128), F32), sb_mask),
            _da_pages(qs_ref, [dnkt_ref], [dnv_ref], True,
                      jnp.full((DA_HEADS * nr, 128), -jnp.inf, F32),
                      jnp.zeros((DA_HEADS * nr, 128), F32),
                      jnp.zeros((DA_HEADS * nr, 128), F32), da_mask))
        c_s[...] = c
        acc_s[...] = acc
        m_s[...] = m
        l_s[...] = l
        dacc_s[...] = dacc

    c, acc = c_s[...], acc_s[...]
    m, l, dacc = m_s[...], l_s[...], dacc_s[...]
    (c, acc), (m, l, dacc) = _run_staged(
        _sb_pages(qt_ref, cache_refs[0::4], cache_refs[1::4], False, tri_t, c, acc, None),
        _da_pages(qs_ref, cache_refs[2::4], cache_refs[3::4], False, m, l, dacc, None))
    c_s[...] = c
    acc_s[...] = acc
    m_s[...] = m
    l_s[...] = l
    dacc_s[...] = dacc

    @pl.when(s == pl.num_programs(1) - 1)
    def _():
        oa_ref[0] = acc
        for h in range(DA_HEADS):
            lam = _da_lambda(lq1, lk1, lq2, lk2, h, lambda_init)
            r0 = slice(h * nr, h * nr + SAMPLE_ROWS)
            r1 = slice(h * nr + SAMPLE_ROWS, (h + 1) * nr)
            oc_ref[0, h] = _da_finish((None, l[r0], dacc[r0]), (None, l[r1], dacc[r1]),
                                      lam, g_ref[...], lambda_init)


def _sample_attn(proj, caches, layer, page_table, lp, nb, ts, lambda_init):
    n_pages = page_table.shape[1]
    npp = PAGES_PER_STEP
    assert n_pages % npp == 0 and ts <= SAMPLE_ROWS
    pad_q = ((0, 0), (0, SAMPLE_ROWS - ts), (0, 0), (0, 0))

    q = jnp.pad(proj[0].reshape(nb, ts, SB_HEADS, SB_HEAD_DIM), pad_q)
    qt = jnp.einsum('bihd,hg->bhdgi', q, jnp.eye(SB_HEADS, dtype=F32))
    qt = qt.reshape(nb, SB_HEADS * SB_HEAD_DIM, SB_HEADS * SAMPLE_ROWS)
    qt = jnp.pad(qt, ((0, 0), (0, 0), (0, 128 - SB_HEADS * SAMPLE_ROWS))).astype(BF16)
    dq = jnp.pad(proj[4].reshape(nb, ts, DA_HEADS, 128), pad_q).transpose(0, 2, 1, 3)
    comp = (jnp.arange(128) // DA_SUB_DIM)[None, :] == jnp.arange(2)[:, None]
    qs = jnp.where(comp[None, None, :, None, :], dq[:, :, None], 0.0)
    qs = qs.reshape(nb, DA_HEADS, 2 * SAMPLE_ROWS, 128).astype(BF16)

    def new_page(g):
        return jnp.pad(proj[g].reshape(nb, ts * SB_HEADS, 128),
                       ((0, 0), (0, PAGE_ROWS - ts * SB_HEADS), (0, 0)))

    dnkt = jnp.pad(proj[5].reshape(nb, ts, 512).transpose(0, 2, 1),
                   ((0, 0), (0, 0), (0, PAGE_SIZE - ts)))

    kern = functools.partial(_sample_attn_kernel, n_new=ts, lambda_init=lambda_init)
    per_seq = lambda b, s, pt: (b, 0, 0)
    page_spec = pl.BlockSpec((1, PAGE_ROWS, 128), per_seq)

    def cache_spec(k):
        return pl.BlockSpec((None, None, PAGE_ROWS, 128),
                            lambda b, s, pt: (layer, pt[b, n_pages - 1 - (s * npp + k)], 0, 0))

    const2 = lambda b, s, pt: (0, 0)
    lam_spec = pl.BlockSpec((DA_HEADS, DA_SUB_DIM), const2)
    cache_specs, cache_args = [], []
    for k in range(npp):
        cache_specs += [cache_spec(k)] * 4
        cache_args += list(caches)
    n_state = DA_HEADS * 2 * SAMPLE_ROWS
    grid_spec = pltpu.PrefetchScalarGridSpec(
        num_scalar_prefetch=1,
        grid=(nb, n_pages // npp),
        in_specs=[pl.BlockSpec((1, SB_HEADS * SB_HEAD_DIM, 128), per_seq),
                  pl.BlockSpec((1, DA_HEADS, 2 * SAMPLE_ROWS, 128), lambda b, s, pt: (b, 0, 0, 0)),
                  page_spec, page_spec, page_spec, page_spec]
                 + cache_specs
                 + [pl.BlockSpec((PAGE_SIZE + 8, PAGE_SIZE), const2),
                    lam_spec, lam_spec, lam_spec, lam_spec,
                    pl.BlockSpec((1, DA_V_DIM), const2)],
        out_specs=[pl.BlockSpec((1, SB_HEADS * SAMPLE_ROWS, 128), per_seq),
                   pl.BlockSpec((1, DA_HEADS, SAMPLE_ROWS, 128), lambda b, s, pt: (b, 0, 0, 0))],
        scratch_shapes=[pltpu.VMEM((SAMPLE_ROWS, 128), F32),
                        pltpu.VMEM((SB_HEADS * SAMPLE_ROWS, 128), F32),
                        pltpu.VMEM((n_state, 128), F32), pltpu.VMEM((n_state, 128), F32),
                        pltpu.VMEM((n_state, 128), F32)],
    )
    oa, oc = pl.pallas_call(
        kern, grid_spec=grid_spec,
        out_shape=[jax.ShapeDtypeStruct((nb, SB_HEADS * SAMPLE_ROWS, 128), F32),
                   jax.ShapeDtypeStruct((nb, DA_HEADS, SAMPLE_ROWS, 128), F32)],
        compiler_params=_params("arbitrary", "arbitrary"),
        name="sample_attn",
    )(page_table, qt, qs, new_page(1), new_page(2), dnkt, new_page(6), *cache_args,
      _tri_t_aug(), lp['da_lam_q1'], lp['da_lam_k1'], lp['da_lam_q2'], lp['da_lam_k2'],
      lp['da_norm_g'].reshape(1, DA_V_DIM))
    a_out = oa.reshape(nb, SB_HEADS, SAMPLE_ROWS, 128)[:, :, :ts].transpose(0, 2, 1, 3)
    c_out = oc[:, :, :ts].transpose(0, 2, 1, 3)
    return a_out.reshape(nb * ts, BRANCH_WIDTH), c_out.reshape(nb * ts, BRANCH_WIDTH)


LANE_GROUP = 512


def _s5_kernel(u_ref, bb_ref, a_ref, h0_ref, cc_ref, d_ref, glu_ref, y_ref, hf_ref,
               bu_s, st_s, *, nseq, steps, emit):
    i = pl.program_id(0)
    n = SSM_LANES

    @pl.when(i == 0)
    def _():
        st_s[...] = h0_ref[...]

    u = u_ref[...]
    bu_s[...] = jnp.dot(u.astype(BF16), bb_ref[...], preferred_element_type=F32)

    for sg in range(nseq // 8):
        for lg in range(n // LANE_GROUP):
            lre = slice(lg * LANE_GROUP, (lg + 1) * LANE_GROUP)
            lim = slice(n + lg * LANE_GROUP, n + (lg + 1) * LANE_GROUP)
            rs = slice(sg * 8, (sg + 1) * 8)
            ar = jnp.broadcast_to(a_ref[0:1, lre], (8, LANE_GROUP))
            ai = jnp.broadcast_to(a_ref[0:1, lim], (8, LANE_GROUP))

            def step(t, carry, lre=lre, lim=lim, sg=sg, ar=ar, ai=ai):
                hr, hi = carry
                r0 = pl.multiple_of(t * nseq + sg * 8, 8)
                nr = ar * hr - ai * hi + bu_s[pl.ds(r0, 8), lre]
                ni = ar * hi + ai * hr + bu_s[pl.ds(r0, 8), lim]
                if emit:
                    bu_s[pl.ds(r0, 8), lre] = nr
                    bu_s[pl.ds(r0, 8), lim] = ni
                return nr, ni

            hr, hi = lax.fori_loop(0, steps, step, (st_s[rs, lre], st_s[rs, lim]))
            st_s[rs, lre] = hr
            st_s[rs, lim] = hi

    if emit:
        y = jnp.dot(bu_s[...].astype(BF16), cc_ref[...], preferred_element_type=F32) + d_ref[...] * u
        y = _gelu_tanh(y)
        gate = jnp.dot(y.astype(BF16), glu_ref[...], preferred_element_type=F32)
        y_ref[...] = y * _sigmoid(gate)
    else:
        y_ref[...] = jnp.zeros(y_ref.shape, F32)

    hf_ref[...] = st_s[...]


def _s5_scan(u_rows, ssm, h0, nseq, steps, emit):
    rows = u_rows.shape[0]
    blk_rows = steps * nseq
    n_chunks = rows // blk_rows
    kern = functools.partial(_s5_kernel, nseq=nseq, steps=steps, emit=emit)
    const = lambda i: (0, 0)
    y_rows = blk_rows if emit else 8
    y, hf = pl.pallas_call(
        kern,
        grid=(n_chunks,),
        in_specs=[
            pl.BlockSpec((blk_rows, SSM_WIDTH), lambda i: (i, 0)),
            pl.BlockSpec((SSM_WIDTH, 2 * SSM_LANES), const),
            pl.BlockSpec((1, 2 * SSM_LANES), const),
            pl.BlockSpec((nseq, 2 * SSM_LANES), const),
            pl.BlockSpec((2 * SSM_LANES, SSM_WIDTH), const),
            pl.BlockSpec((1, SSM_WIDTH), const),
            pl.BlockSpec((SSM_WIDTH, SSM_WIDTH), const),
        ],
        out_specs=[pl.BlockSpec((y_rows, SSM_WIDTH), (lambda i: (i, 0)) if emit else const),
                   pl.BlockSpec((nseq, 2 * SSM_LANES), const)],
        out_shape=[jax.ShapeDtypeStruct((rows if emit else 8, SSM_WIDTH), F32),
                   jax.ShapeDtypeStruct((nseq, 2 * SSM_LANES), F32)],
        scratch_shapes=[pltpu.VMEM((blk_rows, 2 * SSM_LANES), F32),
                        pltpu.VMEM((nseq, 2 * SSM_LANES), F32)],
        compiler_params=_params("arbitrary"),
        name="s5_emit" if emit else "s5_state",
    )(u_rows, ssm['bb'], ssm['a'], h0, ssm['cc'], ssm['d'], ssm['glu'])
    return y, hf


def _segment_init_kernel(a_ref, f_ref, o_ref, *, seg_len, nb, nseg):
    n = SSM_LANES
    pr = a_ref[:, :n]
    pi = a_ref[:, n:]
    rr, ri = None, None
    e = seg_len
    while e:
        if e & 1:
            if rr is None:
                rr, ri = pr, pi
            else:
                rr, ri = rr * pr - ri * pi, rr * pi + ri * pr
        e >>= 1
        if e:
            pr, pi = pr * pr - pi * pi, 2.0 * pr * pi
    for b in range(nb):
        hr = jnp.zeros((1, n), F32)
        hi = jnp.zeros((1, n), F32)
        for k in range(nseg):
            r = b * nseg + k
            o_ref[r:r + 1, :n] = hr
            o_ref[r:r + 1, n:] = hi
            fr = f_ref[r:r + 1, :n]
            fi = f_ref[r:r + 1, n:]
            hr, hi = rr * hr - ri * hi + fr, rr * hi + ri * hr + fi


def _segment_init(a, f, seg_len, nb, nseg):
    kern = functools.partial(_segment_init_kernel, seg_len=seg_len, nb=nb, nseg=nseg)
    return pl.pallas_call(
        kern, out_shape=jax.ShapeDtypeStruct(f.shape, F32), name="s5_segments",
        compiler_params=pltpu.CompilerParams(vmem_limit_bytes=VMEM_LIMIT),
    )(a, f)


def _ssm_params(lp):
    lr = lp['ssm_lam_re']
    li = lp['ssm_lam_im']
    dt = jnp.exp(lp['ssm_log_dt'])[:, None]
    mag = jnp.exp(lr * dt)
    ab_re = mag * jnp.cos(li * dt)
    ab_im = mag * jnp.sin(li * dt)
    den = lr * lr + li * li
    f_re = ((ab_re - 1.0) * lr + ab_im * li) / den
    f_im = (ab_im * lr - (ab_re - 1.0) * li) / den
    b_re = lp['ssm_b_re']
    b_im = lp['ssm_b_im']
    bb_re = f_re[..., None] * b_re - f_im[..., None] * b_im
    bb_im = f_re[..., None] * b_im + f_im[..., None] * b_re
    eye = jnp.eye(SSM_GROUPS, dtype=F32)

    def in_blockdiag(w):
        return jnp.einsum('gnp,gh->gphn', w, eye).reshape(SSM_WIDTH, SSM_LANES)

    def out_blockdiag(w):
        return jnp.einsum('gpn,gh->gnhp', w, eye).reshape(SSM_LANES, SSM_WIDTH)

    return {
        'bb': jnp.concatenate([in_blockdiag(bb_re), in_blockdiag(bb_im)], axis=1).astype(BF16),
        'cc': jnp.concatenate([out_blockdiag(lp['ssm_c_re']), -out_blockdiag(lp['ssm_c_im'])],
                              axis=0).astype(BF16),
        'a': jnp.concatenate([ab_re.reshape(1, SSM_LANES), ab_im.reshape(1, SSM_LANES)], axis=1),
        'd': lp['ssm_d'].reshape(1, SSM_WIDTH),
        'glu': lp['ssm_w_glu'].astype(BF16),
    }


def _s5_prompt(u, ssm, bsz, t, nseg, steps):
    nseq = bsz * nseg
    seg_len = t // nseg
    u_rows = u.reshape(bsz, nseg, seg_len, SSM_WIDTH).transpose(2, 0, 1, 3).reshape(seg_len * nseq, SSM_WIDTH)
    zero = jnp.zeros((nseq, 2 * SSM_LANES), F32)
    _, f = _s5_scan(u_rows, ssm, zero, nseq, steps, emit=False)
    h0 = _segment_init(ssm['a'], f, seg_len, bsz, nseg)
    y_rows, hf = _s5_scan(u_rows, ssm, h0, nseq, steps, emit=True)
    y = y_rows.reshape(seg_len, bsz, nseg, SSM_WIDTH).transpose(1, 2, 0, 3).reshape(bsz * t, SSM_WIDTH)
    last = hf.reshape(bsz, nseg, 2 * SSM_LANES)[:, -1]
    return y, last[:, :SSM_LANES], last[:, SSM_LANES:]


def _s5_sample(u, ssm, h0_re, h0_im, nb, t):
    u_rows = u.reshape(nb, t, SSM_WIDTH).transpose(1, 0, 2).reshape(t * nb, SSM_WIDTH)
    h0 = jnp.concatenate([h0_re.reshape(nb, SSM_LANES), h0_im.reshape(nb, SSM_LANES)], axis=1)
    y_rows, hf = _s5_scan(u_rows, ssm, h0, nb, t, emit=True)
    y = y_rows.reshape(t, nb, SSM_WIDTH).transpose(1, 0, 2).reshape(nb * t, SSM_WIDTH)
    return y, hf[:, :SSM_LANES], hf[:, SSM_LANES:]


def _merge_kernel(x_ref, a_ref, b_ref, c_ref, wg0, wg1, wg2, wb_ref, o_ref):
    x = x_ref[...]
    acc = None
    for n, (br, wg) in enumerate(((a_ref, wg0), (b_ref, wg1), (c_ref, wg2))):
        gate = _sigmoid(jnp.dot(x, wg[...], preferred_element_type=F32))
        pb = jnp.dot(br[...].astype(BF16), wb_ref[n], preferred_element_type=F32)
        acc = gate * pb if acc is None else acc + gate * pb
    o_ref[...] = acc.astype(o_ref.dtype)


def _merge(x_bf, a_out, b_out, c_out, w_in_bf, w_branch_bf, tm, tn):
    m = x_bf.shape[0]
    ncol = D_MODEL // tn
    g0 = GATE_COL0 // tn

    def gate_spec(n):
        return pl.BlockSpec((D_MODEL, tn), lambda i, j: (0, g0 + n * ncol + j))

    br_spec = pl.BlockSpec((tm, BRANCH_WIDTH), lambda i, j: (i, 0))
    return pl.pallas_call(
        _merge_kernel,
        grid=(m // tm, ncol),
        in_specs=[pl.BlockSpec((tm, D_MODEL), lambda i, j: (i, 0)), br_spec, br_spec, br_spec,
                  gate_spec(0), gate_spec(1), gate_spec(2),
                  pl.BlockSpec((N_BRANCH, BRANCH_WIDTH, tn), lambda i, j: (0, 0, j))],
        out_specs=pl.BlockSpec((tm, tn), lambda i, j: (i, j)),
        out_shape=jax.ShapeDtypeStruct((m, D_MODEL), BF16),
        compiler_params=_params("arbitrary", "arbitrary"),
        name="merge",
    )(x_bf, a_out, b_out, c_out, w_in_bf, w_in_bf, w_in_bf, w_branch_bf)


def _layer_norm(v, g, b):
    mu = jnp.mean(v, axis=-1, keepdims=True)
    d = v - mu
    var = jnp.mean(d * d, axis=-1, keepdims=True)
    return d * lax.rsqrt(var + LN_EPS) * g + b


def _out_ln_kernel(m_ref, w_ref, x_ref, g_ref, b_ref, h_ref, hb_ref, *, alpha):
    mix = jnp.dot(m_ref[...], w_ref[...], preferred_element_type=F32)
    h = _layer_norm(alpha * x_ref[...] + mix, g_ref[...], b_ref[...])
    h_ref[...] = h
    hb_ref[...] = h.astype(BF16)


def _out_ln(merged_bf, w_out_bf, x, g, b, alpha, tm):
    m = x.shape[0]
    row = pl.BlockSpec((tm, D_MODEL), lambda i: (i, 0))
    vec = pl.BlockSpec((1, D_MODEL), lambda i: (0, 0))
    return pl.pallas_call(
        functools.partial(_out_ln_kernel, alpha=alpha),
        grid=(m // tm,),
        in_specs=[row, pl.BlockSpec((D_MODEL, D_MODEL), lambda i: (0, 0)), row, vec, vec],
        out_specs=[row, row],
        out_shape=[jax.ShapeDtypeStruct((m, D_MODEL), F32), jax.ShapeDtypeStruct((m, D_MODEL), BF16)],
        compiler_params=_params("arbitrary"),
        name="out_ln",
    )(merged_bf, w_out_bf, x, g.reshape(1, D_MODEL), b.reshape(1, D_MODEL))


def _staircase_pairs(k):
    return [(a, b) for a in range(k) for b in range(k) if (a + 1) * (b + 1) <= k]


def _kth_largest(vals, k):
    neg = -jnp.inf
    cur = functools.reduce(jnp.maximum, vals)
    for _ in range(k - 1):
        nxt = None
        for v in vals:
            c = jnp.where(v < cur, v, neg)
            nxt = c if nxt is None else jnp.maximum(nxt, c)
        cur = nxt
    return cur


def _peer_route_kernel(h_ref, wq_ref, keys_ref, t0_ref, s1_ref, e0_ref, e1_ref,
                       q_s, s0_s, top_s, *, tm):
    q_s[...] = jnp.dot(h_ref[...], wq_ref[...], preferred_element_type=F32).astype(BF16)
    neg = -jnp.inf
    for hd in range(PEER_HEADS):
        for c in range(2):
            col = (hd * 2 + c) * PEER_HALF
            s = lax.dot_general(keys_ref[hd, c], q_s[:, col:col + PEER_HALF], NT_DIMS,
                                preferred_element_type=F32)
            (s0_s if c == 0 else s1_ref)[hd] = s
            cur = jnp.max(s, axis=0, keepdims=True)
            top_s[c, 0, hd:hd + 1, :] = cur
            for a in range(1, PEER_TOPK):
                cur = jnp.max(jnp.where(s < cur, s, neg), axis=0, keepdims=True)
                top_s[c, a, hd:hd + 1, :] = cur
    pairs = _staircase_pairs(PEER_TOPK)
    sums = [top_s[0, a] + top_s[1, b] for a, b in pairs]
    tau = _kth_largest(sums, PEER_TOPK)
    best = top_s[0, 0] + top_s[1, 0]
    z = None
    partner = [None] * PEER_TOPK
    for (a, b), v in zip(pairs, sums):
        sel = v >= tau
        e = jnp.where(sel, jnp.exp(v - best), 0.0)
        z = e if z is None else z + e
        cand = jnp.where(sel, top_s[1, b], jnp.inf)
        partner[a] = cand if partner[a] is None else jnp.minimum(partner[a], cand)
    inv_z = 1.0 / z
    for hd in range(PEER_HEADS):
        row = slice(hd, hd + 1)
        s0 = s0_s[hd]
        t0 = jnp.full(s0.shape, jnp.inf, F32)
        for a in range(PEER_TOPK):
            t0 = jnp.where(s0 == top_s[0, a, row, :], partner[a][row, :], t0)
        t0_ref[hd] = t0
        e0_ref[hd] = jnp.exp(s0 - top_s[0, 0, row, :]) * inv_z[row, :]
        e1_ref[hd] = jnp.exp(s1_ref[hd] - top_s[1, 0, row, :])


def _peer_route(h_bf, wq_bf, keys_bf, tm):
    m = h_bf.shape[0]
    tab = pl.BlockSpec((PEER_HEADS, PEER_KEYS, tm), lambda i: (0, 0, i))
    tab_shape = jax.ShapeDtypeStruct((PEER_HEADS, PEER_KEYS, m), F32)
    return pl.pallas_call(
        functools.partial(_peer_route_kernel, tm=tm),
        grid=(m // tm,),
        in_specs=[pl.BlockSpec((tm, D_MODEL), lambda i: (i, 0)),
                  pl.BlockSpec((D_MODEL, D_MODEL), lambda i: (0, 0)),
                  pl.BlockSpec((PEER_HEADS, 2, PEER_KEYS, PEER_HALF), lambda i: (0, 0, 0, 0))],
        out_specs=[tab, tab, tab, tab],
        out_shape=[tab_shape, tab_shape, tab_shape, tab_shape],
        scratch_shapes=[pltpu.VMEM((tm, D_MODEL), BF16),
                        pltpu.VMEM((PEER_HEADS, PEER_KEYS, tm), F32),
                        pltpu.VMEM((2, PEER_TOPK, PEER_HEADS, tm), F32)],
        compiler_params=_params("arbitrary"),
        name="peer_route",
    )(h_bf, wq_bf, keys_bf)


GATE_ROWS = 16


def _peer_gates(ht_ref, p_ref, t0_ref, e0_ref, s1_ref, e1_ref, eb):
    for ii in range(eb // PEER_KEYS):
        rows_i = [(t0_ref[ii, hd:hd + 1, :], e0_ref[ii, hd:hd + 1, :]) for hd in range(PEER_HEADS)]
        for rt in range(PEER_KEYS // GATE_ROWS):
            r = slice(rt * GATE_ROWS, (rt + 1) * GATE_ROWS)
            w = None
            for hd in range(PEER_HEADS):
                t0, e0 = rows_i[hd]
                sel = jnp.where(s1_ref[hd, r, :] >= t0, e0 * e1_ref[hd, r, :], 0.0)
                w = sel if w is None else w + sel
            ro = slice(ii * PEER_KEYS + rt * GATE_ROWS, ii * PEER_KEYS + (rt + 1) * GATE_ROWS)
            p_ref[ro, :] = (_gelu_tanh(ht_ref[ro, :]) * w).astype(BF16)


def _peer_expert_kernel(h_ref, ua_ref, ub_ref, vta_ref, vtb_ref, t0a_ref, e0a_ref, t0b_ref, e0b_ref,
                        s1_ref, e1_ref, o_ref, ht_0, ht_1, p_0, p_1, *, eb):
    g = pl.program_id(1)
    last = pl.num_programs(1) - 1

    def first_matmul(u_ref, ht_ref):
        ht_ref[...] = lax.dot_general(u_ref[...], h_ref[...], NT_DIMS, preferred_element_type=F32)

    def second_matmul(vt_ref, p_ref):
        o_ref[...] += jnp.dot(vt_ref[...], p_ref[...], preferred_element_type=F32)

    @pl.when(g == 0)
    def _():
        o_ref[...] = jnp.zeros(o_ref.shape, F32)
        ht_1[...] = jnp.zeros(ht_1.shape, F32)
        p_0[...] = jnp.zeros(p_0.shape, BF16)

    @pl.when(g < last)
    def _():
        _peer_gates(ht_1, p_1, t0a_ref, e0a_ref, s1_ref, e1_ref, eb)
        second_matmul(vta_ref, p_0)
        first_matmul(ua_ref, ht_0)

    @pl.when(g + 1 <= last)
    def _():
        _peer_gates(ht_0, p_0, t0b_ref, e0b_ref, s1_ref, e1_ref, eb)
        second_matmul(vtb_ref, p_1)
        first_matmul(ub_ref, ht_1)

    @pl.when(g == last)
    def _():
        _peer_gates(ht_1, p_1, t0a_ref, e0a_ref, s1_ref, e1_ref, eb)
        second_matmul(vta_ref, p_0)
        second_matmul(vtb_ref, p_1)


def _peer_experts(h_bf, u_bf, vt_bf, route, tb, eb):
    m = h_bf.shape[0]
    n_eb = u_bf.shape[0] // eb
    assert n_eb % 2 == 0
    hi = n_eb - 1
    t0, s1, e0, e1 = route
    t0 = t0.transpose(1, 0, 2)
    e0 = e0.transpose(1, 0, 2)
    tab = pl.BlockSpec((PEER_HEADS, PEER_KEYS, tb), lambda t, g: (0, 0, t))
    key_a = pl.BlockSpec((eb // PEER_KEYS, PEER_HEADS, tb), lambda t, g: (jnp.maximum(2 * g - 1, 0), 0, t))
    key_b = pl.BlockSpec((eb // PEER_KEYS, PEER_HEADS, tb), lambda t, g: (jnp.minimum(2 * g, hi), 0, t))
    return pl.pallas_call(
        functools.partial(_peer_expert_kernel, eb=eb),
        grid=(m // tb, n_eb // 2 + 1),
        in_specs=[pl.BlockSpec((tb, D_MODEL), lambda t, g: (t, 0)),
                  pl.BlockSpec((eb, D_MODEL), lambda t, g: (jnp.minimum(2 * g, hi), 0)),
                  pl.BlockSpec((eb, D_MODEL), lambda t, g: (jnp.minimum(2 * g + 1, hi), 0)),
                  pl.BlockSpec((D_MODEL, eb), lambda t, g: (0, jnp.clip(2 * g - 2, 0, hi))),
                  pl.BlockSpec((D_MODEL, eb), lambda t, g: (0, jnp.clip(2 * g - 1, 0, hi))),
                  key_a, key_a, key_b, key_b, tab, tab],
        out_specs=pl.BlockSpec((D_MODEL, tb), lambda t, g: (0, t)),
        out_shape=jax.ShapeDtypeStruct((D_MODEL, m), F32),
        scratch_shapes=[pltpu.VMEM((eb, tb), F32), pltpu.VMEM((eb, tb), F32),
                        pltpu.VMEM((eb, tb), BF16), pltpu.VMEM((eb, tb), BF16)],
        compiler_params=_params("arbitrary", "arbitrary"),
        name="peer_experts",
    )(h_bf, u_bf, u_bf, vt_bf, vt_bf, t0, e0, t0, e0, s1, e1)


def _ffn_ln_kernel(ft_ref, h_ref, g_ref, b_ref, y_ref, yb_ref, *, alpha):
    y = _layer_norm(alpha * h_ref[...] + ft_ref[...].T, g_ref[...], b_ref[...])
    y_ref[...] = y
    yb_ref[...] = y.astype(BF16)


def _ffn_ln(ffn_t, h, g, b, alpha, tm):
    m = h.shape[0]
    row = pl.BlockSpec((tm, D_MODEL), lambda i: (i, 0))
    vec = pl.BlockSpec((1, D_MODEL), lambda i: (0, 0))
    return pl.pallas_call(
        functools.partial(_ffn_ln_kernel, alpha=alpha),
        grid=(m // tm,),
        in_specs=[pl.BlockSpec((D_MODEL, tm), lambda i: (0, i)), row, vec, vec],
        out_specs=[row, row],
        out_shape=[jax.ShapeDtypeStruct((m, D_MODEL), F32), jax.ShapeDtypeStruct((m, D_MODEL), BF16)],
        compiler_params=_params("arbitrary"),
        name="ffn_ln",
    )(ffn_t, h, g.reshape(1, D_MODEL), b.reshape(1, D_MODEL))


def _row_tile(m, pref):
    t = min(m, pref)
    assert m % t == 0, (m, t)
    return t


def _token_mix_tail(x, x_bf, a_out, b_out, c_out, w, lp, alpha):
    m = x.shape[0]
    merged = _merge(x_bf, a_out, b_out, c_out, w['w_in'], w['w_branch'],
                    _row_tile(m, 512), 512)
    h, h_bf = _out_ln(merged, w['w_out'], x, lp['ln1_g'], lp['ln1_b'], alpha, _row_tile(m, 256))
    route = _peer_route(h_bf, w['peer_w_q'], w['peer_keys'], _row_tile(m, 256))
    ffn_t = _peer_experts(h_bf, w['peer_u'], w['peer_vt'], route, _row_tile(m, 512), 512)
    return _ffn_ln(ffn_t, h, lp['ln2_g'], lp['ln2_b'], alpha, _row_tile(m, 256))


def _prompt_layer(x, x_bf, lp, w, ssm, rope_tabs, bsz, t, lambda_init, alpha):
    proj = _project(x_bf, w['w_in'], rope_tabs, _row_tile(bsz * t, 512))
    a_out = _sb_prompt(proj, bsz, t, _row_tile(t, 256))
    c_out = _da_prompt(proj, lp, bsz, t, _row_tile(t, 512), lambda_init)
    nseg = 8 // bsz if 8 % bsz == 0 and t % (8 // bsz) == 0 else 1
    assert (bsz * nseg) % 8 == 0
    seg_len = t // nseg
    b_out, h_re, h_im = _s5_prompt(proj[3], ssm, bsz, t, nseg, _row_tile(seg_len, 64))
    y, y_bf = _token_mix_tail(x, x_bf, a_out, b_out, c_out, w, lp, alpha)
    state = (proj[1], proj[2], proj[5], proj[6], h_re, h_im)
    return y, y_bf, state


def _sample_layer(x, x_bf, lp, w, ssm, rope_tabs, caches, layer, page_table, h0_re, h0_im,
                  nb, t, lambda_init, alpha):
    proj = _project(x_bf, w['w_in'], rope_tabs, nb * t)
    a_out, c_out = _sample_attn(proj, caches, layer, page_table, lp, nb, t, lambda_init)
    b_out, h_re, h_im = _s5_sample(proj[3], ssm, h0_re, h0_im, nb, t)
    y, y_bf = _token_mix_tail(x, x_bf, a_out, b_out, c_out, w, lp, alpha)
    state = (proj[1], proj[2], proj[5], proj[6], h_re, h_im)
    return y, y_bf, state


def kernel(x_prompt, x_sample, cache_sb_k, cache_sb_v, cache_da_k, cache_da_v, state_ssm_re, state_ssm_im, page_table, w_in, w_branch, w_out, ssm_lam_re, ssm_lam_im, ssm_log_dt, ssm_b_re, ssm_b_im, ssm_c_re, ssm_c_im, ssm_d, ssm_w_glu, da_lam_q1, da_lam_k1, da_lam_q2, da_lam_k2, da_norm_g, ln1_g, ln1_b, peer_w_q, peer_keys, peer_u, peer_v, ln2_g, ln2_b):
    depth = w_in.shape[0]
    bsz, t, _ = x_prompt.shape
    nb, ts, _ = x_sample.shape
    n_past = page_table.shape[1] * PAGE_SIZE
    alpha = (2 * depth) ** 0.25
    n_pool = cache_sb_k.shape[1]
    caches = (cache_sb_k.reshape(depth, n_pool, PAGE_ROWS, 128),
              cache_sb_v.reshape(depth, n_pool, PAGE_ROWS, 128),
              jnp.transpose(cache_da_k, (0, 1, 3, 4, 5, 2)).reshape(depth, n_pool, PAGE_ROWS, PAGE_SIZE),
              cache_da_v.reshape(depth, n_pool, PAGE_ROWS, 128))
    rope_p = _rope_tables(jnp.tile(jnp.arange(t, dtype=jnp.int32), bsz))
    rope_s = _rope_tables(jnp.tile(n_past + jnp.arange(ts, dtype=jnp.int32), nb))

    yp = x_prompt.reshape(bsz * t, D_MODEL)
    ys = x_sample.reshape(nb * ts, D_MODEL)
    yp_bf = yp.astype(BF16)
    ys_bf = ys.astype(BF16)
    rows_p, rows_s = [], []
    for l in range(depth):
        lp = {'ssm_lam_re': ssm_lam_re[l], 'ssm_lam_im': ssm_lam_im[l], 'ssm_log_dt': ssm_log_dt[l],
              'ssm_b_re': ssm_b_re[l], 'ssm_b_im': ssm_b_im[l],
              'ssm_c_re': ssm_c_re[l], 'ssm_c_im': ssm_c_im[l],
              'ssm_d': ssm_d[l], 'ssm_w_glu': ssm_w_glu[l],
              'da_lam_q1': da_lam_q1[l], 'da_lam_k1': da_lam_k1[l],
              'da_lam_q2': da_lam_q2[l], 'da_lam_k2': da_lam_k2[l], 'da_norm_g': da_norm_g[l],
              'ln1_g': ln1_g[l], 'ln1_b': ln1_b[l], 'ln2_g': ln2_g[l], 'ln2_b': ln2_b[l]}
        w = {'w_in': w_in[l].astype(BF16), 'w_branch': w_branch[l].astype(BF16),
             'w_out': w_out[l].astype(BF16), 'peer_w_q': peer_w_q[l].astype(BF16),
             'peer_keys': peer_keys[l].astype(BF16), 'peer_u': peer_u[l].astype(BF16),
             'peer_vt': peer_v[l].astype(BF16).T}
        ssm = _ssm_params(lp)
        lambda_init = 0.8 - 0.6 * math.exp(-0.3 * l)
        yp, yp_bf, st_p = _prompt_layer(yp, yp_bf, lp, w, ssm, rope_p, bsz, t, lambda_init, alpha)
        ys, ys_bf, st_s = _sample_layer(ys, ys_bf, lp, w, ssm, rope_s, caches, l, page_table,
                                        state_ssm_re[l], state_ssm_im[l], nb, ts, lambda_init, alpha)
        rows_p.append(st_p)
        rows_s.append(st_s)

    def pack(rows, lead):
        sk, sv, dk, dv, hr, hi = [jnp.stack(z) for z in zip(*rows)]
        n = lead[0]
        return (sk.reshape(depth, *lead, SB_HEADS, SB_HEAD_DIM),
                sv.reshape(depth, *lead, SB_HEADS, SB_HEAD_DIM),
                dk.reshape(depth, *lead, DA_HEADS, 2, DA_SUB_DIM),
                dv.reshape(depth, *lead, DA_HEADS, DA_V_DIM),
                hr.reshape(depth, n, SSM_GROUPS, SSM_STATE),
                hi.reshape(depth, n, SSM_GROUPS, SSM_STATE))

    return ((yp.reshape(bsz, t, D_MODEL), ys.reshape(nb, ts, D_MODEL))
            + pack(rows_p, (bsz, t)) + pack(rows_s, (nb, ts)))
```

```python
import functools
import math

import jax
import jax.numpy as jnp
from jax import lax
from jax.experimental import pallas as pl
from jax.experimental.pallas import tpu as pltpu

F32 = jnp.float32
BF16 = jnp.bfloat16

D_MODEL = 2048
PAGE_SIZE = 128
SB_HEADS = 4
SB_HEAD_DIM = 128
SSM_GROUP = 16
SSM_GROUPS = 32
SSM_STATE = 64
SSM_WIDTH = SSM_GROUPS * SSM_GROUP
SSM_LANES = SSM_GROUPS * SSM_STATE
DA_HEADS = 4
DA_SUB_DIM = 64
DA_V_DIM = 128
ROPE_THETA = 500000.0
ROPE_DIM = DA_SUB_DIM // 4
N_BRANCH = 3
BRANCH_WIDTH = 512
N_PROJ_GROUPS = 7
GATE_COL0 = N_PROJ_GROUPS * BRANCH_WIDTH
PEER_HEADS = 8
PEER_KEYS = 128
PEER_HALF = 128
PEER_TOPK = 16
LN_EPS = 1e-5

V7X_VMEM_BYTES = 64 * 1024 * 1024
VMEM_LIMIT = V7X_VMEM_BYTES - 8 * 1024 * 1024

NT_DIMS = (((1,), (1,)), ((), ()))

F32_EXP_UNDERFLOW = -105.0


def _params(*sem):
    return pltpu.CompilerParams(dimension_semantics=sem, vmem_limit_bytes=VMEM_LIMIT)


def _gelu_tanh(x):
    return 0.5 * x * (1.0 + jnp.tanh(math.sqrt(2.0 / math.pi) * (x + 0.044715 * (x * x * x))))


def _sigmoid(x):
    return 1.0 / (1.0 + jnp.exp(-x))


def _proj_kernel(x_ref, w_ref, cos_ref, sa_ref, sb_ref, o_ref):
    j = pl.program_id(0)
    o_ref[0] = jnp.dot(x_ref[...], w_ref[...], preferred_element_type=F32)

    @pl.when((j == 4) | (j == 5))
    def _():
        c = cos_ref[...]
        sa = sa_ref[...]
        sb = sb_ref[...]
        for t in range(BRANCH_WIDTH // 128):
            xt = o_ref[0, :, t * 128:(t + 1) * 128]
            o_ref[0, :, t * 128:(t + 1) * 128] = (
                xt * c + pltpu.roll(xt, 8, 1) * sa + pltpu.roll(xt, 120, 1) * sb)


def _project(x_bf, w_in_bf, rope_tabs, tm):
    m = x_bf.shape[0]
    nr = m // tm
    cos_t, sa_t, sb_t = rope_tabs

    def tab_map(j, i):
        return (jnp.where((j == 4) | (j == 5), i, 0), 0)

    return pl.pallas_call(
        _proj_kernel,
        grid=(N_PROJ_GROUPS, nr),
        in_specs=[
            pl.BlockSpec((tm, D_MODEL), lambda j, i: (i, 0)),
            pl.BlockSpec((D_MODEL, BRANCH_WIDTH), lambda j, i: (0, j)),
            pl.BlockSpec((tm, 128), tab_map),
            pl.BlockSpec((tm, 128), tab_map),
            pl.BlockSpec((tm, 128), tab_map),
        ],
        out_specs=pl.BlockSpec((1, tm, BRANCH_WIDTH), lambda j, i: (j, i, 0)),
        out_shape=jax.ShapeDtypeStruct((N_PROJ_GROUPS, m, BRANCH_WIDTH), F32),
        compiler_params=_params("arbitrary", "arbitrary"),
        name="proj",
    )(x_bf, w_in_bf, cos_t, sa_t, sb_t)


def _rope_tables(pos):
    half = ROPE_DIM // 2
    inv_freq = ROPE_THETA ** (-jnp.arange(half, dtype=F32) / half)
    ang = pos.astype(F32)[:, None] * inv_freq[None, :]
    cos = jnp.cos(ang)
    sin = jnp.sin(ang)
    n = pos.shape[0]
    ones = jnp.ones((n, DA_SUB_DIM - ROPE_DIM), F32)
    zeros = jnp.zeros((n, DA_SUB_DIM - ROPE_DIM), F32)
    zh = jnp.zeros((n, half), F32)
    c64 = jnp.concatenate([cos, cos, ones], axis=1)
    sa64 = jnp.concatenate([zh, sin, zeros], axis=1)
    sb64 = jnp.concatenate([-sin, zh, zeros], axis=1)
    return tuple(jnp.concatenate([t, t], axis=1) for t in (c64, sa64, sb64))


def _log_sigmoid(z):
    return jnp.minimum(z, 0.0) - jnp.log1p(jnp.exp(-jnp.abs(z)))


def _split_bf16(x):
    hi = x.astype(BF16)
    lo = (x - hi.astype(F32)).astype(BF16)
    return hi, lo


def _sb_block(q, kb, vb, tri, c, acc, scale, mask):
    tk = kb.shape[0]
    z = lax.dot_general(q, kb, NT_DIMS, preferred_element_type=F32) * scale
    lb = _log_sigmoid(z)
    lk = lb - z
    if mask is not None:
        lk = jnp.where(mask, lk, 0.0)
    hi, lo = _split_bf16(lk)
    r = (jnp.dot(hi, tri, preferred_element_type=F32)
         + jnp.dot(lo, tri, preferred_element_type=F32))
    w = jnp.exp(lb + r[:, :tk] + c)
    if mask is not None:
        w = jnp.where(mask, w, 0.0)
    acc = acc + jnp.dot(w.astype(BF16), vb, preferred_element_type=F32)
    return c + r[:, tk:], acc


def _sb_prompt_kernel(q_ref, k_ref, v_ref, tri_ref, o_ref, kbf, vbf, *, blk, scale):
    qi = pl.program_id(2)

    @pl.when(qi == 0)
    def _():
        kbf[...] = k_ref[0].astype(BF16)
        vbf[...] = v_ref[0].astype(BF16)

    q = q_ref[0].astype(BF16)
    tri = tri_ref[...]
    row = lax.broadcasted_iota(jnp.int32, (blk, blk), 0)
    col = lax.broadcasted_iota(jnp.int32, (blk, blk), 1)
    off = pl.multiple_of(qi * blk, blk)
    c0 = jnp.zeros((blk, blk), F32)
    a0 = jnp.zeros((blk, SB_HEAD_DIM), F32)
    c, acc = _sb_block(q, kbf[pl.ds(off, blk), :], vbf[pl.ds(off, blk), :], tri, c0, a0,
                       scale, col < row)

    def cond(carry):
        jj, _, _, cmax = carry
        return (jj < qi) & (cmax > F32_EXP_UNDERFLOW)

    def body(carry):
        jj, c, acc, _ = carry
        o = pl.multiple_of((qi - 1 - jj) * blk, blk)
        c, acc = _sb_block(q, kbf[pl.ds(o, blk), :], vbf[pl.ds(o, blk), :], tri, c, acc, scale, None)
        return jj + 1, c, acc, jnp.max(c[:, :128])

    _, _, acc, _ = lax.while_loop(cond, body, (jnp.int32(0), c, acc, jnp.max(c[:, :128])))
    o_ref[...] = acc


def _tri_ones(blk):
    j = jnp.arange(blk)[:, None]
    s = jnp.arange(blk)[None, :]
    return jnp.concatenate([(j > s).astype(BF16), jnp.ones((blk, blk), BF16)], axis=1)


def _sb_prompt(proj, bsz, t, blk):
    nq = t // blk
    kern = functools.partial(_sb_prompt_kernel, blk=blk, scale=SB_HEAD_DIM ** -0.5)
    return pl.pallas_call(
        kern,
        grid=(bsz, SB_HEADS, nq),
        in_specs=[
            pl.BlockSpec((1, blk, SB_HEAD_DIM), lambda b, h, i: (0, b * nq + i, h)),
            pl.BlockSpec((1, t, SB_HEAD_DIM), lambda b, h, i: (1, b, h)),
            pl.BlockSpec((1, t, SB_HEAD_DIM), lambda b, h, i: (2, b, h)),
            pl.BlockSpec((blk, 2 * blk), lambda b, h, i: (0, 0)),
        ],
        out_specs=pl.BlockSpec((blk, SB_HEAD_DIM), lambda b, h, i: (b * nq + i, h)),
        out_shape=jax.ShapeDtypeStruct((bsz * t, SB_HEADS * SB_HEAD_DIM), F32),
        scratch_shapes=[pltpu.VMEM((t, SB_HEAD_DIM), BF16), pltpu.VMEM((t, SB_HEAD_DIM), BF16)],
        compiler_params=_params("arbitrary", "arbitrary", "arbitrary"),
        name="sb_prompt",
    )(proj, proj, proj, _tri_ones(blk))


def _da_lambda(lq1, lk1, lq2, lk2, h, lambda_init):
    a = jnp.sum(lq1[pl.ds(h, 1), :] * lk1[pl.ds(h, 1), :], axis=1, keepdims=True)
    b = jnp.sum(lq2[pl.ds(h, 1), :] * lk2[pl.ds(h, 1), :], axis=1, keepdims=True)
    return jnp.exp(a) - jnp.exp(b) + lambda_init


def _split_components(q):
    lane = lax.broadcasted_iota(jnp.int32, q.shape, 1)
    q0 = jnp.where(lane < DA_SUB_DIM, q, 0.0).astype(BF16)
    q1 = jnp.where(lane >= DA_SUB_DIM, q, 0.0).astype(BF16)
    return q0, q1


def _softmax_block(qc, kb, vb, m, l, acc, scale, mask):
    s = lax.dot_general(qc, kb, NT_DIMS, preferred_element_type=F32) * scale
    yield
    if mask is not None:
        s = jnp.where(mask, s, -jnp.inf)
    m_new = jnp.maximum(m, jnp.max(s, axis=1, keepdims=True))
    p = jnp.exp(s - m_new)
    alpha = jnp.exp(m - m_new)
    l = alpha * l + jnp.sum(p, axis=1, keepdims=True)
    p = p.astype(BF16)
    yield
    acc = alpha * acc + jnp.dot(p, vb, preferred_element_type=F32)
    return m_new, l, acc


def _da_finish(st0, st1, lam, g, lambda_init):
    o = st0[2] / st0[1] - lam * (st1[2] / st1[1])
    o = o * lax.rsqrt(jnp.mean(o * o, axis=-1, keepdims=True) + LN_EPS) * g
    return (1.0 - lambda_init) * o


def _da_prompt_kernel(q_ref, k_ref, v_ref, lq1, lk1, lq2, lk2, g_ref, o_ref, kbf, vbf,
                      *, blk, scale, lambda_init):
    h = pl.program_id(1)
    qi = pl.program_id(2)

    @pl.when(qi == 0)
    def _():
        kbf[...] = k_ref[0].astype(BF16)
        vbf[...] = v_ref[0].astype(BF16)

    q0, q1 = _split_components(q_ref[0])
    row = lax.broadcasted_iota(jnp.int32, (blk, blk), 0)
    col = lax.broadcasted_iota(jnp.int32, (blk, blk), 1)
    mask = col <= row
    off = pl.multiple_of(qi * blk, blk)
    init = (jnp.full((blk, 1), -jnp.inf, F32), jnp.zeros((blk, 1), F32),
            jnp.zeros((blk, DA_V_DIM), F32))
    kd = kbf[pl.ds(off, blk), :]
    vd = vbf[pl.ds(off, blk), :]
    st0, st1 = _run_staged(_softmax_block(q0, kd, vd, *init, scale, mask),
                           _softmax_block(q1, kd, vd, *init, scale, mask))

    def body(j, carry):
        o = pl.multiple_of(j * blk, blk)
        kb = kbf[pl.ds(o, blk), :]
        vb = vbf[pl.ds(o, blk), :]
        return tuple(_run_staged(_softmax_block(q0, kb, vb, *carry[0], scale, None),
                                 _softmax_block(q1, kb, vb, *carry[1], scale, None)))

    st0, st1 = lax.fori_loop(0, qi, body, (st0, st1))
    lam = _da_lambda(lq1, lk1, lq2, lk2, h, lambda_init)
    o_ref[...] = _da_finish(st0, st1, lam, g_ref[...], lambda_init)


def _da_prompt(proj, lp, bsz, t, blk, lambda_init):
    nq = t // blk
    kern = functools.partial(_da_prompt_kernel, blk=blk, scale=DA_SUB_DIM ** -0.5,
                             lambda_init=lambda_init)
    lam_spec = pl.BlockSpec((DA_HEADS, DA_SUB_DIM), lambda b, h, i: (0, 0))
    return pl.pallas_call(
        kern,
        grid=(bsz, DA_HEADS, nq),
        in_specs=[
            pl.BlockSpec((1, blk, 128), lambda b, h, i: (4, b * nq + i, h)),
            pl.BlockSpec((1, t, 128), lambda b, h, i: (5, b, h)),
            pl.BlockSpec((1, t, DA_V_DIM), lambda b, h, i: (6, b, h)),
            lam_spec, lam_spec, lam_spec, lam_spec,
            pl.BlockSpec((1, DA_V_DIM), lambda b, h, i: (0, 0)),
        ],
        out_specs=pl.BlockSpec((blk, DA_V_DIM), lambda b, h, i: (b * nq + i, h)),
        out_shape=jax.ShapeDtypeStruct((bsz * t, DA_HEADS * DA_V_DIM), F32),
        scratch_shapes=[pltpu.VMEM((t, 128), BF16), pltpu.VMEM((t, DA_V_DIM), BF16)],
        compiler_params=_params("arbitrary", "arbitrary", "arbitrary"),
        name="da_prompt",
    )(proj, proj, proj, lp['da_lam_q1'], lp['da_lam_k1'], lp['da_lam_q2'], lp['da_lam_k2'],
      lp['da_norm_g'].reshape(1, DA_V_DIM))


SAMPLE_ROWS = 8
PAGES_PER_STEP = 8
PAGE_ROWS = PAGE_SIZE * SB_HEADS


def _tri_t_aug():
    s = jnp.arange(PAGE_SIZE + 8)[:, None]
    j = jnp.arange(PAGE_SIZE)[None, :]
    return ((j > s) | (s >= PAGE_SIZE)).astype(BF16)


def _head_rows(refs, h, lead):
    idx = (pl.ds(h, PAGE_SIZE, stride=SB_HEADS), slice(None))
    parts = [(ref[(0,) + idx] if lead else ref[idx]).astype(BF16) for ref in refs]
    return parts[0] if len(parts) == 1 else jnp.concatenate(parts, axis=0)


def _run_staged(*gens):
    results = [None] * len(gens)
    live = list(range(len(gens)))
    while live:
        for idx in list(live):
            try:
                next(gens[idx])
            except StopIteration as stop:
                results[idx] = stop.value
                live.remove(idx)
    return results


def _sb_pages(qt_ref, k_refs, v_refs, lead, tri_t, c, acc, mask):
    z = None
    for h in range(SB_HEADS):
        d = jnp.dot(_head_rows(k_refs, h, lead), qt_ref[0, h * 128:(h + 1) * 128, :],
                    preferred_element_type=F32)
        z = d if z is None else z + d
    yield
    z = z * (SB_HEAD_DIM ** -0.5)
    lb = _log_sigmoid(z)
    lk = lb - z
    if mask is not None:
        lk = jnp.where(mask, lk, 0.0)
    hi, lo = _split_bf16(lk)
    yield
    later = []
    for k in range(len(k_refs)):
        rows = slice(k * PAGE_SIZE, (k + 1) * PAGE_SIZE)
        r = (jnp.dot(tri_t, hi[rows], preferred_element_type=F32)
             + jnp.dot(tri_t, lo[rows], preferred_element_type=F32))
        later.append(r[:PAGE_SIZE] + c[0:1, :])
        c = c + r[PAGE_SIZE:]
    yield
    later = later[0] if len(later) == 1 else jnp.concatenate(later, axis=0)
    w = jnp.exp(lb + later)
    if mask is not None:
        w = jnp.where(mask, w, 0.0)
    wt = w.T
    yield
    parts = []
    for h in range(SB_HEADS):
        rows = slice(h * SAMPLE_ROWS, (h + 1) * SAMPLE_ROWS)
        parts.append(acc[rows] + jnp.dot(wt[rows].astype(BF16), _head_rows(v_refs, h, lead),
                                         preferred_element_type=F32))
    return c, jnp.concatenate(parts, axis=0)


def _da_pages(qs_ref, kt_refs, v_refs, lead, m, l, acc, mask):
    scale = DA_SUB_DIM ** -0.5
    nr = 2 * SAMPLE_ROWS
    scores = []
    for h in range(DA_HEADS):
        hs = slice(h * 128, (h + 1) * 128)
        kts = [(ref[0, hs, :] if lead else ref[hs, :]).astype(BF16) for ref in kt_refs]
        kt = kts[0] if len(kts) == 1 else jnp.concatenate(kts, axis=1)
        scores.append(jnp.dot(qs_ref[0, h], kt, preferred_element_type=F32) * scale)
    yield
    ms, ls, ps, alphas = [], [], [], []
    for h in range(DA_HEADS):
        rows = slice(h * nr, (h + 1) * nr)
        s = scores[h]
        if mask is not None:
            s = jnp.where(mask, s, -jnp.inf)
        m_old = m[rows]
        m_new = jnp.maximum(m_old, jnp.max(s, axis=1, keepdims=True))
        p = jnp.exp(s - m_new[:, :1])
        alphas.append(jnp.exp(m_old - m_new))
        ms.append(m_new)
        ls.append(alphas[h] * l[rows] + jnp.sum(p, axis=1, keepdims=True))
        ps.append(p.astype(BF16))
    yield
    accs = []
    for h in range(DA_HEADS):
        rows = slice(h * nr, (h + 1) * nr)
        accs.append(alphas[h] * acc[rows] + jnp.dot(ps[h], _head_rows(v_refs, h, lead),
                                                    preferred_element_type=F32))
    return (jnp.concatenate(ms, axis=0), jnp.concatenate(ls, axis=0), jnp.concatenate(accs, axis=0))


def _sample_attn_kernel(pt_ref, qt_ref, qs_ref, nk_ref, nv_ref, dnkt_ref, dnv_ref, *rest,
                        n_new, lambda_init):
    del pt_ref
    npp = PAGES_PER_STEP
    cache_refs = rest[:4 * npp]
    tri_ref, lq1, lk1, lq2, lk2, g_ref, oa_ref, oc_ref, c_s, acc_s, m_s, l_s, dacc_s = rest[4 * npp:]
    s = pl.program_id(1)
    tri_t = tri_ref[...]
    nr = 2 * SAMPLE_ROWS

    @pl.when(s == 0)
    def _():
        key = lax.broadcasted_iota(jnp.int32, (PAGE_SIZE, 128), 0)
        qry = lax.broadcasted_iota(jnp.int32, (PAGE_SIZE, 128), 1) % SAMPLE_ROWS
        sb_mask = (key < n_new) & (key < qry)
        qry = lax.broadcasted_iota(jnp.int32, (nr, PAGE_SIZE), 0) % SAMPLE_ROWS
        key = lax.broadcasted_iota(jnp.int32, (nr, PAGE_SIZE), 1)
        da_mask = (key < n_new) & (key <= qry)
        (c, acc), (m, l, dacc) = _run_staged(
            _sb_pages(qt_ref, [nk_ref], [nv_ref], True, tri_t,
                      jnp.zeros((SAMPLE_ROWS, 128), F32),
                      jnp.zeros((SB_HEADS * SAMPLE_ROWS, 128), F32), sb_mask),
            _da_pages(qs_ref, [dnkt_ref], [dnv_ref], True,
                      jnp.full((DA_HEADS * nr, 128), -jnp.inf, F32),
                      jnp.zeros((DA_HEADS * nr, 128), F32),
                      jnp.zeros((DA_HEADS * nr, 128), F32), da_mask))
        c_s[...] = c
        acc_s[...] = acc
        m_s[...] = m
        l_s[...] = l
        dacc_s[...] = dacc

    c, acc = c_s[...], acc_s[...]
    m, l, dacc = m_s[...], l_s[...], dacc_s[...]
    (c, acc), (m, l, dacc) = _run_staged(
        _sb_pages(qt_ref, cache_refs[0::4], cache_refs[1::4], False, tri_t, c, acc, None),
        _da_pages(qs_ref, cache_refs[2::4], cache_refs[3::4], False, m, l, dacc, None))
    c_s[...] = c
    acc_s[...] = acc
    m_s[...] = m
    l_s[...] = l
    dacc_s[...] = dacc

    @pl.when(s == pl.num_programs(1) - 1)
    def _():
        oa_ref[0] = acc
        for h in range(DA_HEADS):
            lam = _da_lambda(lq1, lk1, lq2, lk2, h, lambda_init)
            r0 = slice(h * nr, h * nr + SAMPLE_ROWS)
            r1 = slice(h * nr + SAMPLE_ROWS, (h + 1) * nr)
            oc_ref[0, h] = _da_finish((None, l[r0], dacc[r0]), (None, l[r1], dacc[r1]),
                                      lam, g_ref[...], lambda_init)


def _sample_attn(proj, caches, layer, page_table, lp, nb, ts, lambda_init):
    n_pages = page_table.shape[1]
    npp = PAGES_PER_STEP
    assert n_pages % npp == 0 and ts <= SAMPLE_ROWS
    pad_q = ((0, 0), (0, SAMPLE_ROWS - ts), (0, 0), (0, 0))

    q = jnp.pad(proj[0].reshape(nb, ts, SB_HEADS, SB_HEAD_DIM), pad_q)
    qt = jnp.einsum('bihd,hg->bhdgi', q, jnp.eye(SB_HEADS, dtype=F32))
    qt = qt.reshape(nb, SB_HEADS * SB_HEAD_DIM, SB_HEADS * SAMPLE_ROWS)
    qt = jnp.pad(qt, ((0, 0), (0, 0), (0, 128 - SB_HEADS * SAMPLE_ROWS))).astype(BF16)
    dq = jnp.pad(proj[4].reshape(nb, ts, DA_HEADS, 128), pad_q).transpose(0, 2, 1, 3)
    comp = (jnp.arange(128) // DA_SUB_DIM)[None, :] == jnp.arange(2)[:, None]
    qs = jnp.where(comp[None, None, :, None, :], dq[:, :, None], 0.0)
    qs = qs.reshape(nb, DA_HEADS, 2 * SAMPLE_ROWS, 128).astype(BF16)

    def new_page(g):
        return jnp.pad(proj[g].reshape(nb, ts * SB_HEADS, 128),
                       ((0, 0), (0, PAGE_ROWS - ts * SB_HEADS), (0, 0)))

    dnkt = jnp.pad(proj[5].reshape(nb, ts, 512).transpose(0, 2, 1),
                   ((0, 0), (0, 0), (0, PAGE_SIZE - ts)))

    kern = functools.partial(_sample_attn_kernel, n_new=ts, lambda_init=lambda_init)
    per_seq = lambda b, s, pt: (b, 0, 0)
    page_spec = pl.BlockSpec((1, PAGE_ROWS, 128), per_seq)

    def cache_spec(k):
        return pl.BlockSpec((None, None, PAGE_ROWS, 128),
                            lambda b, s, pt: (layer, pt[b, n_pages - 1 - (s * npp + k)], 0, 0))

    const2 = lambda b, s, pt: (0, 0)
    lam_spec = pl.BlockSpec((DA_HEADS, DA_SUB_DIM), const2)
    cache_specs, cache_args = [], []
    for k in range(npp):
        cache_specs += [cache_spec(k)] * 4
        cache_args += list(caches)
    n_state = DA_HEADS * 2 * SAMPLE_ROWS
    grid_spec = pltpu.PrefetchScalarGridSpec(
        num_scalar_prefetch=1,
        grid=(nb, n_pages // npp),
        in_specs=[pl.BlockSpec((1, SB_HEADS * SB_HEAD_DIM, 128), per_seq),
                  pl.BlockSpec((1, DA_HEADS, 2 * SAMPLE_ROWS, 128), lambda b, s, pt: (b, 0, 0, 0)),
                  page_spec, page_spec, page_spec, page_spec]
                 + cache_specs
                 + [pl.BlockSpec((PAGE_SIZE + 8, PAGE_SIZE), const2),
                    lam_spec, lam_spec, lam_spec, lam_spec,
                    pl.BlockSpec((1, DA_V_DIM), const2)],
        out_specs=[pl.BlockSpec((1, SB_HEADS * SAMPLE_ROWS, 128), per_seq),
                   pl.BlockSpec((1, DA_HEADS, SAMPLE_ROWS, 128), lambda b, s, pt: (b, 0, 0, 0))],
        scratch_shapes=[pltpu.VMEM((SAMPLE_ROWS, 128), F32),
                        pltpu.VMEM((SB_HEADS * SAMPLE_ROWS, 128), F32),
                        pltpu.VMEM((n_state, 128), F32), pltpu.VMEM((n_state, 128), F32),
                        pltpu.VMEM((n_state, 128), F32)],
    )
    oa, oc = pl.pallas_call(
        kern, grid_spec=grid_spec,
        out_shape=[jax.ShapeDtypeStruct((nb, SB_HEADS * SAMPLE_ROWS, 128), F32),
                   jax.ShapeDtypeStruct((nb, DA_HEADS, SAMPLE_ROWS, 128), F32)],
        compiler_params=_params("arbitrary", "arbitrary"),
        name="sample_attn",
    )(page_table, qt, qs, new_page(1), new_page(2), dnkt, new_page(6), *cache_args,
      _tri_t_aug(), lp['da_lam_q1'], lp['da_lam_k1'], lp['da_lam_q2'], lp['da_lam_k2'],
      lp['da_norm_g'].reshape(1, DA_V_DIM))
    a_out = oa.reshape(nb, SB_HEADS, SAMPLE_ROWS, 128)[:, :, :ts].transpose(0, 2, 1, 3)
    c_out = oc[:, :, :ts].transpose(0, 2, 1, 3)
    return a_out.reshape(nb * ts, BRANCH_WIDTH), c_out.reshape(nb * ts, BRANCH_WIDTH)


LANE_GROUP = 512


def _s5_kernel(u_ref, bb_ref, a_ref, h0_ref, cc_ref, d_ref, glu_ref, y_ref, hf_ref,
               bu_s, st_s, *, nseq, steps, emit):
    i = pl.program_id(0)
    n = SSM_LANES

    @pl.when(i == 0)
    def _():
        st_s[...] = h0_ref[...]

    u = u_ref[...]
    bu_s[...] = jnp.dot(u.astype(BF16), bb_ref[...], preferred_element_type=F32)

    for sg in range(nseq // 8):
        for lg in range(n // LANE_GROUP):
            lre = slice(lg * LANE_GROUP, (lg + 1) * LANE_GROUP)
            lim = slice(n + lg * LANE_GROUP, n + (lg + 1) * LANE_GROUP)
            rs = slice(sg * 8, (sg + 1) * 8)
            ar = jnp.broadcast_to(a_ref[0:1, lre], (8, LANE_GROUP))
            ai = jnp.broadcast_to(a_ref[0:1, lim], (8, LANE_GROUP))

            def step(t, carry, lre=lre, lim=lim, sg=sg, ar=ar, ai=ai):
                hr, hi = carry
                r0 = pl.multiple_of(t * nseq + sg * 8, 8)
                nr = ar * hr - ai * hi + bu_s[pl.ds(r0, 8), lre]
                ni = ar * hi + ai * hr + bu_s[pl.ds(r0, 8), lim]
                if emit:
                    bu_s[pl.ds(r0, 8), lre] = nr
                    bu_s[pl.ds(r0, 8), lim] = ni
                return nr, ni

            hr, hi = lax.fori_loop(0, steps, step, (st_s[rs, lre], st_s[rs, lim]))
            st_s[rs, lre] = hr
            st_s[rs, lim] = hi

    if emit:
        y = jnp.dot(bu_s[...].astype(BF16), cc_ref[...], preferred_element_type=F32) + d_ref[...] * u
        y = _gelu_tanh(y)
        gate = jnp.dot(y.astype(BF16), glu_ref[...], preferred_element_type=F32)
        y_ref[...] = y * _sigmoid(gate)
    else:
        y_ref[...] = jnp.zeros(y_ref.shape, F32)

    hf_ref[...] = st_s[...]


def _s5_scan(u_rows, ssm, h0, nseq, steps, emit):
    rows = u_rows.shape[0]
    blk_rows = steps * nseq
    n_chunks = rows // blk_rows
    kern = functools.partial(_s5_kernel, nseq=nseq, steps=steps, emit=emit)
    const = lambda i: (0, 0)
    y_rows = blk_rows if emit else 8
    y, hf = pl.pallas_call(
        kern,
        grid=(n_chunks,),
        in_specs=[
            pl.BlockSpec((blk_rows, SSM_WIDTH), lambda i: (i, 0)),
            pl.BlockSpec((SSM_WIDTH, 2 * SSM_LANES), const),
            pl.BlockSpec((1, 2 * SSM_LANES), const),
            pl.BlockSpec((nseq, 2 * SSM_LANES), const),
            pl.BlockSpec((2 * SSM_LANES, SSM_WIDTH), const),
            pl.BlockSpec((1, SSM_WIDTH), const),
            pl.BlockSpec((SSM_WIDTH, SSM_WIDTH), const),
        ],
        out_specs=[pl.BlockSpec((y_rows, SSM_WIDTH), (lambda i: (i, 0)) if emit else const),
                   pl.BlockSpec((nseq, 2 * SSM_LANES), const)],
        out_shape=[jax.ShapeDtypeStruct((rows if emit else 8, SSM_WIDTH), F32),
                   jax.ShapeDtypeStruct((nseq, 2 * SSM_LANES), F32)],
        scratch_shapes=[pltpu.VMEM((blk_rows, 2 * SSM_LANES), F32),
                        pltpu.VMEM((nseq, 2 * SSM_LANES), F32)],
        compiler_params=_params("arbitrary"),
        name="s5_emit" if emit else "s5_state",
    )(u_rows, ssm['bb'], ssm['a'], h0, ssm['cc'], ssm['d'], ssm['glu'])
    return y, hf


def _segment_init_kernel(a_ref, f_ref, o_ref, *, seg_len, nb, nseg):
    n = SSM_LANES
    pr = a_ref[:, :n]
    pi = a_ref[:, n:]
    rr, ri = None, None
    e = seg_len
    while e:
        if e & 1:
            if rr is None:
                rr, ri = pr, pi
            else:
                rr, ri = rr * pr - ri * pi, rr * pi + ri * pr
        e >>= 1
        if e:
            pr, pi = pr * pr - pi * pi, 2.0 * pr * pi
    for b in range(nb):
        hr = jnp.zeros((1, n), F32)
        hi = jnp.zeros((1, n), F32)
        for k in range(nseg):
            r = b * nseg + k
            o_ref[r:r + 1, :n] = hr
            o_ref[r:r + 1, n:] = hi
            fr = f_ref[r:r + 1, :n]
            fi = f_ref[r:r + 1, n:]
            hr, hi = rr * hr - ri * hi + fr, rr * hi + ri * hr + fi


def _segment_init(a, f, seg_len, nb, nseg):
    kern = functools.partial(_segment_init_kernel, seg_len=seg_len, nb=nb, nseg=nseg)
    return pl.pallas_call(
        kern, out_shape=jax.ShapeDtypeStruct(f.shape, F32), name="s5_segments",
        compiler_params=pltpu.CompilerParams(vmem_limit_bytes=VMEM_LIMIT),
    )(a, f)


def _ssm_params(lp):
    lr = lp['ssm_lam_re']
    li = lp['ssm_lam_im']
    dt = jnp.exp(lp['ssm_log_dt'])[:, None]
    mag = jnp.exp(lr * dt)
    ab_re = mag * jnp.cos(li * dt)
    ab_im = mag * jnp.sin(li * dt)
    den = lr * lr + li * li
    f_re = ((ab_re - 1.0) * lr + ab_im * li) / den
    f_im = (ab_im * lr - (ab_re - 1.0) * li) / den
    b_re = lp['ssm_b_re']
    b_im = lp['ssm_b_im']
    bb_re = f_re[..., None] * b_re - f_im[..., None] * b_im
    bb_im = f_re[..., None] * b_im + f_im[..., None] * b_re
    eye = jnp.eye(SSM_GROUPS, dtype=F32)

    def in_blockdiag(w):
        return jnp.einsum('gnp,gh->gphn', w, eye).reshape(SSM_WIDTH, SSM_LANES)

    def out_blockdiag(w):
        return jnp.einsum('gpn,gh->gnhp', w, eye).reshape(SSM_LANES, SSM_WIDTH)

    return {
        'bb': jnp.concatenate([in_blockdiag(bb_re), in_blockdiag(bb_im)], axis=1).astype(BF16),
        'cc': jnp.concatenate([out_blockdiag(lp['ssm_c_re']), -out_blockdiag(lp['ssm_c_im'])],
                              axis=0).astype(BF16),
        'a': jnp.concatenate([ab_re.reshape(1, SSM_LANES), ab_im.reshape(1, SSM_LANES)], axis=1),
        'd': lp['ssm_d'].reshape(1, SSM_WIDTH),
        'glu': lp['ssm_w_glu'].astype(BF16),
    }


def _s5_prompt(u, ssm, bsz, t, nseg, steps):
    nseq = bsz * nseg
    seg_len = t // nseg
    u_rows = u.reshape(bsz, nseg, seg_len, SSM_WIDTH).transpose(2, 0, 1, 3).reshape(seg_len * nseq, SSM_WIDTH)
    zero = jnp.zeros((nseq, 2 * SSM_LANES), F32)
    _, f = _s5_scan(u_rows, ssm, zero, nseq, steps, emit=False)
    h0 = _segment_init(ssm['a'], f, seg_len, bsz, nseg)
    y_rows, hf = _s5_scan(u_rows, ssm, h0, nseq, steps, emit=True)
    y = y_rows.reshape(seg_len, bsz, nseg, SSM_WIDTH).transpose(1, 2, 0, 3).reshape(bsz * t, SSM_WIDTH)
    last = hf.reshape(bsz, nseg, 2 * SSM_LANES)[:, -1]
    return y, last[:, :SSM_LANES], last[:, SSM_LANES:]


def _s5_sample(u, ssm, h0_re, h0_im, nb, t):
    u_rows = u.reshape(nb, t, SSM_WIDTH).transpose(1, 0, 2).reshape(t * nb, SSM_WIDTH)
    h0 = jnp.concatenate([h0_re.reshape(nb, SSM_LANES), h0_im.reshape(nb, SSM_LANES)], axis=1)
    y_rows, hf = _s5_scan(u_rows, ssm, h0, nb, t, emit=True)
    y = y_rows.reshape(t, nb, SSM_WIDTH).transpose(1, 0, 2).reshape(nb * t, SSM_WIDTH)
    return y, hf[:, :SSM_LANES], hf[:, SSM_LANES:]


def _merge_kernel(x_ref, a_ref, b_ref, c_ref, wg0, wg1, wg2, wb_ref, o_ref):
    x = x_ref[...]
    acc = None
    for n, (br, wg) in enumerate(((a_ref, wg0), (b_ref, wg1), (c_ref, wg2))):
        gate = _sigmoid(jnp.dot(x, wg[...], preferred_element_type=F32))
        pb = jnp.dot(br[...].astype(BF16), wb_ref[n], preferred_element_type=F32)
        acc = gate * pb if acc is None else acc + gate * pb
    o_ref[...] = acc.astype(o_ref.dtype)


def _merge(x_bf, a_out, b_out, c_out, w_in_bf, w_branch_bf, tm, tn):
    m = x_bf.shape[0]
    ncol = D_MODEL // tn
    g0 = GATE_COL0 // tn

    def gate_spec(n):
        return pl.BlockSpec((D_MODEL, tn), lambda i, j: (0, g0 + n * ncol + j))

    br_spec = pl.BlockSpec((tm, BRANCH_WIDTH), lambda i, j: (i, 0))
    return pl.pallas_call(
        _merge_kernel,
        grid=(m // tm, ncol),
        in_specs=[pl.BlockSpec((tm, D_MODEL), lambda i, j: (i, 0)), br_spec, br_spec, br_spec,
                  gate_spec(0), gate_spec(1), gate_spec(2),
                  pl.BlockSpec((N_BRANCH, BRANCH_WIDTH, tn), lambda i, j: (0, 0, j))],
        out_specs=pl.BlockSpec((tm, tn), lambda i, j: (i, j)),
        out_shape=jax.ShapeDtypeStruct((m, D_MODEL), BF16),
        compiler_params=_params("arbitrary", "arbitrary"),
        name="merge",
    )(x_bf, a_out, b_out, c_out, w_in_bf, w_in_bf, w_in_bf, w_branch_bf)


def _layer_norm(v, g, b):
    mu = jnp.mean(v, axis=-1, keepdims=True)
    d = v - mu
    var = jnp.mean(d * d, axis=-1, keepdims=True)
    return d * lax.rsqrt(var + LN_EPS) * g + b


def _out_ln_kernel(m_ref, w_ref, x_ref, g_ref, b_ref, h_ref, hb_ref, *, alpha):
    mix = jnp.dot(m_ref[...], w_ref[...], preferred_element_type=F32)
    h = _layer_norm(alpha * x_ref[...] + mix, g_ref[...], b_ref[...])
    h_ref[...] = h
    hb_ref[...] = h.astype(BF16)


def _out_ln(merged_bf, w_out_bf, x, g, b, alpha, tm):
    m = x.shape[0]
    row = pl.BlockSpec((tm, D_MODEL), lambda i: (i, 0))
    vec = pl.BlockSpec((1, D_MODEL), lambda i: (0, 0))
    return pl.pallas_call(
        functools.partial(_out_ln_kernel, alpha=alpha),
        grid=(m // tm,),
        in_specs=[row, pl.BlockSpec((D_MODEL, D_MODEL), lambda i: (0, 0)), row, vec, vec],
        out_specs=[row, row],
        out_shape=[jax.ShapeDtypeStruct((m, D_MODEL), F32), jax.ShapeDtypeStruct((m, D_MODEL), BF16)],
        compiler_params=_params("arbitrary"),
        name="out_ln",
    )(merged_bf, w_out_bf, x, g.reshape(1, D_MODEL), b.reshape(1, D_MODEL))


def _staircase_pairs(k):
    return [(a, b) for a in range(k) for b in range(k) if (a + 1) * (b + 1) <= k]


def _kth_largest(vals, k):
    neg = -jnp.inf
    cur = functools.reduce(jnp.maximum, vals)
    for _ in range(k - 1):
        nxt = None
        for v in vals:
            c = jnp.where(v < cur, v, neg)
            nxt = c if nxt is None else jnp.maximum(nxt, c)
        cur = nxt
    return cur


NOT_RANKED = 127.0


def _peer_route_kernel(h_ref, wq_ref, keys_ref, cnt0_ref, rank1_ref, e0_ref, e1_ref,
                       q_s, s0_s, s1_s, top_s, *, tm):
    q_s[...] = jnp.dot(h_ref[...], wq_ref[...], preferred_element_type=F32).astype(BF16)
    neg = -jnp.inf
    for hd in range(PEER_HEADS):
        for c in range(2):
            col = (hd * 2 + c) * PEER_HALF
            s = lax.dot_general(keys_ref[hd, c], q_s[:, col:col + PEER_HALF], NT_DIMS,
                                preferred_element_type=F32)
            (s0_s if c == 0 else s1_s)[hd] = s
            cur = jnp.max(s, axis=0, keepdims=True)
            top_s[c, 0, hd:hd + 1, :] = cur
            for a in range(1, PEER_TOPK):
                cur = jnp.max(jnp.where(s < cur, s, neg), axis=0, keepdims=True)
                top_s[c, a, hd:hd + 1, :] = cur
    pairs = _staircase_pairs(PEER_TOPK)
    sums = [top_s[0, a] + top_s[1, b] for a, b in pairs]
    tau = _kth_largest(sums, PEER_TOPK)
    best = top_s[0, 0] + top_s[1, 0]
    z = None
    partners = [None] * PEER_TOPK
    for (a, b), v in zip(pairs, sums):
        sel = v >= tau
        e = jnp.where(sel, jnp.exp(v - best), 0.0)
        z = e if z is None else z + e
        one = jnp.where(sel, 1.0, 0.0)
        partners[a] = one if partners[a] is None else partners[a] + one
    inv_z = 1.0 / z
    for hd in range(PEER_HEADS):
        row = slice(hd, hd + 1)
        s0 = s0_s[hd]
        s1 = s1_s[hd]
        cnt0 = jnp.zeros(s0.shape, F32)
        rank1 = jnp.full(s1.shape, NOT_RANKED, F32)
        for a in range(PEER_TOPK):
            cnt0 = jnp.where(s0 == top_s[0, a, row, :], partners[a][row, :], cnt0)
            rank1 = jnp.where(s1 == top_s[1, a, row, :], float(a), rank1)
        cnt0_ref[hd] = cnt0
        rank1_ref[hd] = rank1.astype(BF16)
        e0_ref[hd] = jnp.exp(s0 - top_s[0, 0, row, :]) * inv_z[row, :]
        e1_ref[hd] = jnp.exp(s1 - top_s[1, 0, row, :]).astype(BF16)


def _peer_route(h_bf, wq_bf, keys_bf, tm):
    m = h_bf.shape[0]
    tab = pl.BlockSpec((PEER_HEADS, PEER_KEYS, tm), lambda i: (0, 0, i))
    tab_shape = jax.ShapeDtypeStruct((PEER_HEADS, PEER_KEYS, m), F32)
    tab_shape_bf = jax.ShapeDtypeStruct((PEER_HEADS, PEER_KEYS, m), BF16)
    return pl.pallas_call(
        functools.partial(_peer_route_kernel, tm=tm),
        grid=(m // tm,),
        in_specs=[pl.BlockSpec((tm, D_MODEL), lambda i: (i, 0)),
                  pl.BlockSpec((D_MODEL, D_MODEL), lambda i: (0, 0)),
                  pl.BlockSpec((PEER_HEADS, 2, PEER_KEYS, PEER_HALF), lambda i: (0, 0, 0, 0))],
        out_specs=[tab, tab, tab, tab],
        out_shape=[tab_shape, tab_shape_bf, tab_shape, tab_shape_bf],
        scratch_shapes=[pltpu.VMEM((tm, D_MODEL), BF16),
                        pltpu.VMEM((PEER_HEADS, PEER_KEYS, tm), F32),
                        pltpu.VMEM((PEER_HEADS, PEER_KEYS, tm), F32),
                        pltpu.VMEM((2, PEER_TOPK, PEER_HEADS, tm), F32)],
        compiler_params=_params("arbitrary"),
        name="peer_route",
    )(h_bf, wq_bf, keys_bf)


GATE_ROWS = 16


def _peer_gates(ht_ref, p_ref, cnt0_ref, e0_ref, rank1_ref, e1_ref, eb):
    tile = (GATE_ROWS, ht_ref.shape[1])
    zero = jnp.zeros((), BF16)
    for ii in range(eb // PEER_KEYS):
        rows_i = [(jnp.broadcast_to(cnt0_ref[ii, hd:hd + 1, :], tile).astype(BF16),
                   jnp.broadcast_to(e0_ref[ii, hd:hd + 1, :], tile).astype(BF16))
                  for hd in range(PEER_HEADS)]
        for rt in range(PEER_KEYS // GATE_ROWS):
            r = slice(rt * GATE_ROWS, (rt + 1) * GATE_ROWS)
            w = None
            for hd in range(PEER_HEADS):
                cnt0, e0 = rows_i[hd]
                sel = jnp.where(rank1_ref[hd, r, :] < cnt0, e0 * e1_ref[hd, r, :], zero)
                w = sel if w is None else w + sel
            ro = slice(ii * PEER_KEYS + rt * GATE_ROWS, ii * PEER_KEYS + (rt + 1) * GATE_ROWS)
            p_ref[ro, :] = _gelu_tanh(ht_ref[ro, :]).astype(BF16) * w


def _peer_expert_kernel(h_ref, ua_ref, ub_ref, vta_ref, vtb_ref, t0a_ref, e0a_ref, t0b_ref, e0b_ref,
                        s1_ref, e1_ref, o_ref, ht_0, ht_1, p_0, p_1, *, eb):
    g = pl.program_id(1)
    last = pl.num_programs(1) - 1

    def first_matmul(u_ref, ht_ref):
        ht_ref[...] = lax.dot_general(u_ref[...], h_ref[...], NT_DIMS, preferred_element_type=F32)

    def second_matmul(vt_ref, p_ref):
        o_ref[...] += jnp.dot(vt_ref[...], p_ref[...], preferred_element_type=F32)

    @pl.when(g == 0)
    def _():
        o_ref[...] = jnp.zeros(o_ref.shape, F32)
        ht_1[...] = jnp.zeros(ht_1.shape, F32)
        p_0[...] = jnp.zeros(p_0.shape, BF16)

    @pl.when(g < last)
    def _():
        _peer_gates(ht_1, p_1, t0a_ref, e0a_ref, s1_ref, e1_ref, eb)
        second_matmul(vta_ref, p_0)
        first_matmul(ua_ref, ht_0)

    @pl.when(g + 1 <= last)
    def _():
        _peer_gates(ht_0, p_0, t0b_ref, e0b_ref, s1_ref, e1_ref, eb)
        second_matmul(vtb_ref, p_1)
        first_matmul(ub_ref, ht_1)

    @pl.when(g == last)
    def _():
        _peer_gates(ht_1, p_1, t0a_ref, e0a_ref, s1_ref, e1_ref, eb)
        second_matmul(vta_ref, p_0)
        second_matmul(vtb_ref, p_1)


def _peer_experts(h_bf, u_bf, vt_bf, route, tb, eb):
    m = h_bf.shape[0]
    n_eb = u_bf.shape[0] // eb
    assert n_eb % 2 == 0
    hi = n_eb - 1
    cnt0, rank1, e0, e1 = route
    cnt0 = cnt0.transpose(1, 0, 2)
    e0 = e0.transpose(1, 0, 2)
    tab = pl.BlockSpec((PEER_HEADS, PEER_KEYS, tb), lambda t, g: (0, 0, t))
    key_a = pl.BlockSpec((eb // PEER_KEYS, PEER_HEADS, tb), lambda t, g: (jnp.maximum(2 * g - 1, 0), 0, t))
    key_b = pl.BlockSpec((eb // PEER_KEYS, PEER_HEADS, tb), lambda t, g: (jnp.minimum(2 * g, hi), 0, t))
    return pl.pallas_call(
        functools.partial(_peer_expert_kernel, eb=eb),
        grid=(m // tb, n_eb // 2 + 1),
        in_specs=[pl.BlockSpec((tb, D_MODEL), lambda t, g: (t, 0)),
                  pl.BlockSpec((eb, D_MODEL), lambda t, g: (jnp.minimum(2 * g, hi), 0)),
                  pl.BlockSpec((eb, D_MODEL), lambda t, g: (jnp.minimum(2 * g + 1, hi), 0)),
                  pl.BlockSpec((D_MODEL, eb), lambda t, g: (0, jnp.clip(2 * g - 2, 0, hi))),
                  pl.BlockSpec((D_MODEL, eb), lambda t, g: (0, jnp.clip(2 * g - 1, 0, hi))),
                  key_a, key_a, key_b, key_b, tab, tab],
        out_specs=pl.BlockSpec((D_MODEL, tb), lambda t, g: (0, t)),
        out_shape=jax.ShapeDtypeStruct((D_MODEL, m), F32),
        scratch_shapes=[pltpu.VMEM((eb, tb), F32), pltpu.VMEM((eb, tb), F32),
                        pltpu.VMEM((eb, tb), BF16), pltpu.VMEM((eb, tb), BF16)],
        compiler_params=_params("arbitrary", "arbitrary"),
        name="peer_experts",
    )(h_bf, u_bf, u_bf, vt_bf, vt_bf, cnt0, e0, cnt0, e0, rank1, e1)


def _ffn_ln_kernel(ft_ref, h_ref, g_ref, b_ref, y_ref, yb_ref, *, alpha):
    y = _layer_norm(alpha * h_ref[...] + ft_ref[...].T, g_ref[...], b_ref[...])
    y_ref[...] = y
    yb_ref[...] = y.astype(BF16)


def _ffn_ln(ffn_t, h, g, b, alpha, tm):
    m = h.shape[0]
    row = pl.BlockSpec((tm, D_MODEL), lambda i: (i, 0))
    vec = pl.BlockSpec((1, D_MODEL), lambda i: (0, 0))
    return pl.pallas_call(
        functools.partial(_ffn_ln_kernel, alpha=alpha),
        grid=(m // tm,),
        in_specs=[pl.BlockSpec((D_MODEL, tm), lambda i: (0, i)), row, vec, vec],
        out_specs=[row, row],
        out_shape=[jax.ShapeDtypeStruct((m, D_MODEL), F32), jax.ShapeDtypeStruct((m, D_MODEL), BF16)],
        compiler_params=_params("arbitrary"),
        name="ffn_ln",
    )(ffn_t, h, g.reshape(1, D_MODEL), b.reshape(1, D_MODEL))


def _row_tile(m, pref):
    t = min(m, pref)
    assert m % t == 0, (m, t)
    return t


def _token_mix_tail(x, x_bf, a_out, b_out, c_out, w, lp, alpha):
    m = x.shape[0]
    merged = _merge(x_bf, a_out, b_out, c_out, w['w_in'], w['w_branch'],
                    _row_tile(m, 512), 512)
    h, h_bf = _out_ln(merged, w['w_out'], x, lp['ln1_g'], lp['ln1_b'], alpha, _row_tile(m, 256))
    route = _peer_route(h_bf, w['peer_w_q'], w['peer_keys'], _row_tile(m, 256))
    ffn_t = _peer_experts(h_bf, w['peer_u'], w['peer_vt'], route, _row_tile(m, 512), 512)
    return _ffn_ln(ffn_t, h, lp['ln2_g'], lp['ln2_b'], alpha, _row_tile(m, 256))


def _prompt_layer(x, x_bf, lp, w, ssm, rope_tabs, bsz, t, lambda_init, alpha):
    proj = _project(x_bf, w['w_in'], rope_tabs, _row_tile(bsz * t, 1024))
    a_out = _sb_prompt(proj, bsz, t, _row_tile(t, 256))
    c_out = _da_prompt(proj, lp, bsz, t, _row_tile(t, 512), lambda_init)
    nseg = 8 // bsz if 8 % bsz == 0 and t % (8 // bsz) == 0 else 1
    assert (bsz * nseg) % 8 == 0
    seg_len = t // nseg
    b_out, h_re, h_im = _s5_prompt(proj[3], ssm, bsz, t, nseg, _row_tile(seg_len, 64))
    y, y_bf = _token_mix_tail(x, x_bf, a_out, b_out, c_out, w, lp, alpha)
    state = (proj[1], proj[2], proj[5], proj[6], h_re, h_im)
    return y, y_bf, state


def _sample_layer(x, x_bf, lp, w, ssm, rope_tabs, caches, layer, page_table, h0_re, h0_im,
                  nb, t, lambda_init, alpha):
    proj = _project(x_bf, w['w_in'], rope_tabs, nb * t)
    a_out, c_out = _sample_attn(proj, caches, layer, page_table, lp, nb, t, lambda_init)
    b_out, h_re, h_im = _s5_sample(proj[3], ssm, h0_re, h0_im, nb, t)
    y, y_bf = _token_mix_tail(x, x_bf, a_out, b_out, c_out, w, lp, alpha)
    state = (proj[1], proj[2], proj[5], proj[6], h_re, h_im)
    return y, y_bf, state


def kernel(x_prompt, x_sample, cache_sb_k, cache_sb_v, cache_da_k, cache_da_v, state_ssm_re, state_ssm_im, page_table, w_in, w_branch, w_out, ssm_lam_re, ssm_lam_im, ssm_log_dt, ssm_b_re, ssm_b_im, ssm_c_re, ssm_c_im, ssm_d, ssm_w_glu, da_lam_q1, da_lam_k1, da_lam_q2, da_lam_k2, da_norm_g, ln1_g, ln1_b, peer_w_q, peer_keys, peer_u, peer_v, ln2_g, ln2_b):
    depth = w_in.shape[0]
    bsz, t, _ = x_prompt.shape
    nb, ts, _ = x_sample.shape
    n_past = page_table.shape[1] * PAGE_SIZE
    alpha = (2 * depth) ** 0.25
    n_pool = cache_sb_k.shape[1]
    caches = (cache_sb_k.reshape(depth, n_pool, PAGE_ROWS, 128),
              cache_sb_v.reshape(depth, n_pool, PAGE_ROWS, 128),
              jnp.transpose(cache_da_k, (0, 1, 3, 4, 5, 2)).reshape(depth, n_pool, PAGE_ROWS, PAGE_SIZE),
              cache_da_v.reshape(depth, n_pool, PAGE_ROWS, 128))
    rope_p = _rope_tables(jnp.tile(jnp.arange(t, dtype=jnp.int32), bsz))
    rope_s = _rope_tables(jnp.tile(n_past + jnp.arange(ts, dtype=jnp.int32), nb))

    yp = x_prompt.reshape(bsz * t, D_MODEL)
    ys = x_sample.reshape(nb * ts, D_MODEL)
    yp_bf = yp.astype(BF16)
    ys_bf = ys.astype(BF16)
    rows_p, rows_s = [], []
    for l in range(depth):
        lp = {'ssm_lam_re': ssm_lam_re[l], 'ssm_lam_im': ssm_lam_im[l], 'ssm_log_dt': ssm_log_dt[l],
              'ssm_b_re': ssm_b_re[l], 'ssm_b_im': ssm_b_im[l],
              'ssm_c_re': ssm_c_re[l], 'ssm_c_im': ssm_c_im[l],
              'ssm_d': ssm_d[l], 'ssm_w_glu': ssm_w_glu[l],
              'da_lam_q1': da_lam_q1[l], 'da_lam_k1': da_lam_k1[l],
              'da_lam_q2': da_lam_q2[l], 'da_lam_k2': da_lam_k2[l], 'da_norm_g': da_norm_g[l],
              'ln1_g': ln1_g[l], 'ln1_b': ln1_b[l], 'ln2_g': ln2_g[l], 'ln2_b': ln2_b[l]}
        w = {'w_in': w_in[l].astype(BF16), 'w_branch': w_branch[l].astype(BF16),
             'w_out': w_out[l].astype(BF16), 'peer_w_q': peer_w_q[l].astype(BF16),
             'peer_keys': peer_keys[l].astype(BF16), 'peer_u': peer_u[l].astype(BF16),
             'peer_vt': peer_v[l].astype(BF16).T}
        ssm = _ssm_params(lp)
        lambda_init = 0.8 - 0.6 * math.exp(-0.3 * l)
        yp, yp_bf, st_p = _prompt_layer(yp, yp_bf, lp, w, ssm, rope_p, bsz, t, lambda_init, alpha)
        ys, ys_bf, st_s = _sample_layer(ys, ys_bf, lp, w, ssm, rope_s, caches, l, page_table,
                                        state_ssm_re[l], state_ssm_im[l], nb, ts, lambda_init, alpha)
        rows_p.append(st_p)
        rows_s.append(st_s)

    def pack(rows, lead):
        sk, sv, dk, dv, hr, hi = [jnp.stack(z) for z in zip(*rows)]
        n = lead[0]
        return (sk.reshape(depth, *lead, SB_HEADS, SB_HEAD_DIM),
                sv.reshape(depth, *lead, SB_HEADS, SB_HEAD_DIM),
                dk.reshape(depth, *lead, DA_HEADS, 2, DA_SUB_DIM),
                dv.reshape(depth, *lead, DA_HEADS, DA_V_DIM),
                hr.reshape(depth, n, SSM_GROUPS, SSM_STATE),
                hi.reshape(depth, n, SSM_GROUPS, SSM_STATE))

    return ((yp.reshape(bsz, t, D_MODEL), ys.reshape(nb, ts, D_MODEL))
            + pack(rows_p, (bsz, t)) + pack(rows_s, (nb, ts)))
```

```python
import functools
import math

import jax
import jax.numpy as jnp
from jax import lax
from jax.experimental import pallas as pl
from jax.experimental.pallas import tpu as pltpu

F32 = jnp.float32
BF16 = jnp.bfloat16

D_MODEL = 2048
PAGE_SIZE = 128
SB_HEADS = 4
SB_HEAD_DIM = 128
SSM_GROUP = 16
SSM_GROUPS = 32
SSM_STATE = 64
SSM_WIDTH = SSM_GROUPS * SSM_GROUP
SSM_LANES = SSM_GROUPS * SSM_STATE
DA_HEADS = 4
DA_SUB_DIM = 64
DA_V_DIM = 128
ROPE_THETA = 500000.0
ROPE_DIM = DA_SUB_DIM // 4
N_BRANCH = 3
BRANCH_WIDTH = 512
N_PROJ_GROUPS = 7
GATE_COL0 = N_PROJ_GROUPS * BRANCH_WIDTH
PEER_HEADS = 8
PEER_KEYS = 128
PEER_HALF = 128
PEER_TOPK = 16
LN_EPS = 1e-5

V7X_VMEM_BYTES = 64 * 1024 * 1024
VMEM_LIMIT = V7X_VMEM_BYTES - 8 * 1024 * 1024

NT_DIMS = (((1,), (1,)), ((), ()))

F32_EXP_UNDERFLOW = -105.0


def _params(*sem):
    return pltpu.CompilerParams(dimension_semantics=sem, vmem_limit_bytes=VMEM_LIMIT)


def _gelu_tanh(x):
    return 0.5 * x * (1.0 + jnp.tanh(math.sqrt(2.0 / math.pi) * (x + 0.044715 * (x * x * x))))


def _sigmoid(x):
    return 1.0 / (1.0 + jnp.exp(-x))


def _proj_kernel(x_ref, w_ref, cos_ref, sa_ref, sb_ref, o_ref):
    j = pl.program_id(0)
    o_ref[0] = jnp.dot(x_ref[...], w_ref[...], preferred_element_type=F32)

    @pl.when((j == 4) | (j == 5))
    def _():
        c = cos_ref[...]
        sa = sa_ref[...]
        sb = sb_ref[...]
        for t in range(BRANCH_WIDTH // 128):
            xt = o_ref[0, :, t * 128:(t + 1) * 128]
            o_ref[0, :, t * 128:(t + 1) * 128] = (
                xt * c + pltpu.roll(xt, 8, 1) * sa + pltpu.roll(xt, 120, 1) * sb)


def _project(x_bf, w_in_bf, rope_tabs, tm):
    m = x_bf.shape[0]
    nr = m // tm
    cos_t, sa_t, sb_t = rope_tabs

    def tab_map(j, i):
        return (jnp.where((j == 4) | (j == 5), i, 0), 0)

    return pl.pallas_call(
        _proj_kernel,
        grid=(N_PROJ_GROUPS, nr),
        in_specs=[
            pl.BlockSpec((tm, D_MODEL), lambda j, i: (i, 0)),
            pl.BlockSpec((D_MODEL, BRANCH_WIDTH), lambda j, i: (0, j)),
            pl.BlockSpec((tm, 128), tab_map),
            pl.BlockSpec((tm, 128), tab_map),
            pl.BlockSpec((tm, 128), tab_map),
        ],
        out_specs=pl.BlockSpec((1, tm, BRANCH_WIDTH), lambda j, i: (j, i, 0)),
        out_shape=jax.ShapeDtypeStruct((N_PROJ_GROUPS, m, BRANCH_WIDTH), F32),
        compiler_params=_params("arbitrary", "arbitrary"),
        name="proj",
    )(x_bf, w_in_bf, cos_t, sa_t, sb_t)


def _rope_tables(pos):
    half = ROPE_DIM // 2
    inv_freq = ROPE_THETA ** (-jnp.arange(half, dtype=F32) / half)
    ang = pos.astype(F32)[:, None] * inv_freq[None, :]
    cos = jnp.cos(ang)
    sin = jnp.sin(ang)
    n = pos.shape[0]
    ones = jnp.ones((n, DA_SUB_DIM - ROPE_DIM), F32)
    zeros = jnp.zeros((n, DA_SUB_DIM - ROPE_DIM), F32)
    zh = jnp.zeros((n, half), F32)
    c64 = jnp.concatenate([cos, cos, ones], axis=1)
    sa64 = jnp.concatenate([zh, sin, zeros], axis=1)
    sb64 = jnp.concatenate([-sin, zh, zeros], axis=1)
    return tuple(jnp.concatenate([t, t], axis=1) for t in (c64, sa64, sb64))


def _log_sigmoid(z):
    return jnp.minimum(z, 0.0) - jnp.log1p(jnp.exp(-jnp.abs(z)))


def _split_bf16(x):
    hi = x.astype(BF16)
    lo = (x - hi.astype(F32)).astype(BF16)
    return hi, lo


def _sb_block(q, kb, vb, tri, c, acc, scale, mask):
    tk = kb.shape[0]
    z = lax.dot_general(q, kb, NT_DIMS, preferred_element_type=F32) * scale
    lb = _log_sigmoid(z)
    lk = lb - z
    if mask is not None:
        lk = jnp.where(mask, lk, 0.0)
    hi, lo = _split_bf16(lk)
    r = (jnp.dot(hi, tri, preferred_element_type=F32)
         + jnp.dot(lo, tri, preferred_element_type=F32))
    w = jnp.exp(lb + r[:, :tk] + c)
    if mask is not None:
        w = jnp.where(mask, w, 0.0)
    acc = acc + jnp.dot(w.astype(BF16), vb, preferred_element_type=F32)
    return c + r[:, tk:], acc


def _sb_prompt_kernel(q_ref, k_ref, v_ref, tri_ref, o_ref, kbf, vbf, *, blk, scale):
    qi = pl.program_id(2)

    @pl.when(qi == 0)
    def _():
        kbf[...] = k_ref[0].astype(BF16)
        vbf[...] = v_ref[0].astype(BF16)

    q = q_ref[0].astype(BF16)
    tri = tri_ref[...]
    row = lax.broadcasted_iota(jnp.int32, (blk, blk), 0)
    col = lax.broadcasted_iota(jnp.int32, (blk, blk), 1)
    off = pl.multiple_of(qi * blk, blk)
    c0 = jnp.zeros((blk, blk), F32)
    a0 = jnp.zeros((blk, SB_HEAD_DIM), F32)
    c, acc = _sb_block(q, kbf[pl.ds(off, blk), :], vbf[pl.ds(off, blk), :], tri, c0, a0,
                       scale, col < row)

    def cond(carry):
        jj, _, _, cmax = carry
        return (jj < qi) & (cmax > F32_EXP_UNDERFLOW)

    def body(carry):
        jj, c, acc, _ = carry
        o = pl.multiple_of((qi - 1 - jj) * blk, blk)
        c, acc = _sb_block(q, kbf[pl.ds(o, blk), :], vbf[pl.ds(o, blk), :], tri, c, acc, scale, None)
        return jj + 1, c, acc, jnp.max(c[:, :128])

    _, _, acc, _ = lax.while_loop(cond, body, (jnp.int32(0), c, acc, jnp.max(c[:, :128])))
    o_ref[...] = acc


def _tri_ones(blk):
    j = jnp.arange(blk)[:, None]
    s = jnp.arange(blk)[None, :]
    return jnp.concatenate([(j > s).astype(BF16), jnp.ones((blk, blk), BF16)], axis=1)


def _sb_prompt(proj, bsz, t, blk):
    nq = t // blk
    kern = functools.partial(_sb_prompt_kernel, blk=blk, scale=SB_HEAD_DIM ** -0.5)
    return pl.pallas_call(
        kern,
        grid=(bsz, SB_HEADS, nq),
        in_specs=[
            pl.BlockSpec((1, blk, SB_HEAD_DIM), lambda b, h, i: (0, b * nq + i, h)),
            pl.BlockSpec((1, t, SB_HEAD_DIM), lambda b, h, i: (1, b, h)),
            pl.BlockSpec((1, t, SB_HEAD_DIM), lambda b, h, i: (2, b, h)),
            pl.BlockSpec((blk, 2 * blk), lambda b, h, i: (0, 0)),
        ],
        out_specs=pl.BlockSpec((blk, SB_HEAD_DIM), lambda b, h, i: (b * nq + i, h)),
        out_shape=jax.ShapeDtypeStruct((bsz * t, SB_HEADS * SB_HEAD_DIM), F32),
        scratch_shapes=[pltpu.VMEM((t, SB_HEAD_DIM), BF16), pltpu.VMEM((t, SB_HEAD_DIM), BF16)],
        compiler_params=_params("arbitrary", "arbitrary", "arbitrary"),
        name="sb_prompt",
    )(proj, proj, proj, _tri_ones(blk))


def _da_lambda(lq1, lk1, lq2, lk2, h, lambda_init):
    a = jnp.sum(lq1[pl.ds(h, 1), :] * lk1[pl.ds(h, 1), :], axis=1, keepdims=True)
    b = jnp.sum(lq2[pl.ds(h, 1), :] * lk2[pl.ds(h, 1), :], axis=1, keepdims=True)
    return jnp.exp(a) - jnp.exp(b) + lambda_init


def _split_components(q):
    lane = lax.broadcasted_iota(jnp.int32, q.shape, 1)
    q0 = jnp.where(lane < DA_SUB_DIM, q, 0.0).astype(BF16)
    q1 = jnp.where(lane >= DA_SUB_DIM, q, 0.0).astype(BF16)
    return q0, q1


def _softmax_block(qc, kb, vb, m, l, acc, scale, mask):
    s = lax.dot_general(qc, kb, NT_DIMS, preferred_element_type=F32) * scale
    yield
    if mask is not None:
        s = jnp.where(mask, s, -jnp.inf)
    m_new = jnp.maximum(m, jnp.max(s, axis=1, keepdims=True))
    p = jnp.exp(s - m_new)
    alpha = jnp.exp(m - m_new)
    l = alpha * l + jnp.sum(p, axis=1, keepdims=True)
    p = p.astype(BF16)
    yield
    acc = alpha * acc + jnp.dot(p, vb, preferred_element_type=F32)
    return m_new, l, acc


def _da_finish(st0, st1, lam, g, lambda_init):
    o = st0[2] / st0[1] - lam * (st1[2] / st1[1])
    o = o * lax.rsqrt(jnp.mean(o * o, axis=-1, keepdims=True) + LN_EPS) * g
    return (1.0 - lambda_init) * o


def _da_prompt_kernel(q_ref, k_ref, v_ref, lq1, lk1, lq2, lk2, g_ref, o_ref, kbf, vbf,
                      *, blk, scale, lambda_init):
    h = pl.program_id(1)
    qi = pl.program_id(2)

    @pl.when(qi == 0)
    def _():
        kbf[...] = k_ref[0].astype(BF16)
        vbf[...] = v_ref[0].astype(BF16)

    q0, q1 = _split_components(q_ref[0])
    row = lax.broadcasted_iota(jnp.int32, (blk, blk), 0)
    col = lax.broadcasted_iota(jnp.int32, (blk, blk), 1)
    mask = col <= row
    off = pl.multiple_of(qi * blk, blk)
    init = (jnp.full((blk, 1), -jnp.inf, F32), jnp.zeros((blk, 1), F32),
            jnp.zeros((blk, DA_V_DIM), F32))
    kd = kbf[pl.ds(off, blk), :]
    vd = vbf[pl.ds(off, blk), :]
    st0, st1 = _run_staged(_softmax_block(q0, kd, vd, *init, scale, mask),
                           _softmax_block(q1, kd, vd, *init, scale, mask))

    def body(j, carry):
        o = pl.multiple_of(j * blk, blk)
        kb = kbf[pl.ds(o, blk), :]
        vb = vbf[pl.ds(o, blk), :]
        return tuple(_run_staged(_softmax_block(q0, kb, vb, *carry[0], scale, None),
                                 _softmax_block(q1, kb, vb, *carry[1], scale, None)))

    st0, st1 = lax.fori_loop(0, qi, body, (st0, st1))
    lam = _da_lambda(lq1, lk1, lq2, lk2, h, lambda_init)
    o_ref[...] = _da_finish(st0, st1, lam, g_ref[...], lambda_init)


def _da_prompt(proj, lp, bsz, t, blk, lambda_init):
    nq = t // blk
    kern = functools.partial(_da_prompt_kernel, blk=blk, scale=DA_SUB_DIM ** -0.5,
                             lambda_init=lambda_init)
    lam_spec = pl.BlockSpec((DA_HEADS, DA_SUB_DIM), lambda b, h, i: (0, 0))
    return pl.pallas_call(
        kern,
        grid=(bsz, DA_HEADS, nq),
        in_specs=[
            pl.BlockSpec((1, blk, 128), lambda b, h, i: (4, b * nq + i, h)),
            pl.BlockSpec((1, t, 128), lambda b, h, i: (5, b, h)),
            pl.BlockSpec((1, t, DA_V_DIM), lambda b, h, i: (6, b, h)),
            lam_spec, lam_spec, lam_spec, lam_spec,
            pl.BlockSpec((1, DA_V_DIM), lambda b, h, i: (0, 0)),
        ],
        out_specs=pl.BlockSpec((blk, DA_V_DIM), lambda b, h, i: (b * nq + i, h)),
        out_shape=jax.ShapeDtypeStruct((bsz * t, DA_HEADS * DA_V_DIM), F32),
        scratch_shapes=[pltpu.VMEM((t, 128), BF16), pltpu.VMEM((t, DA_V_DIM), BF16)],
        compiler_params=_params("arbitrary", "arbitrary", "arbitrary"),
        name="da_prompt",
    )(proj, proj, proj, lp['da_lam_q1'], lp['da_lam_k1'], lp['da_lam_q2'], lp['da_lam_k2'],
      lp['da_norm_g'].reshape(1, DA_V_DIM))


SAMPLE_ROWS = 8
PAGES_PER_STEP = 16
PAGE_ROWS = PAGE_SIZE * SB_HEADS


def _tri_t_aug():
    s = jnp.arange(PAGE_SIZE + 8)[:, None]
    j = jnp.arange(PAGE_SIZE)[None, :]
    return ((j > s) | (s >= PAGE_SIZE)).astype(BF16)


def _head_rows(refs, h, lead):
    idx = (pl.ds(h, PAGE_SIZE, stride=SB_HEADS), slice(None))
    parts = [(ref[(0,) + idx] if lead else ref[idx]).astype(BF16) for ref in refs]
    return parts[0] if len(parts) == 1 else jnp.concatenate(parts, axis=0)


def _run_staged(*gens):
    results = [None] * len(gens)
    live = list(range(len(gens)))
    while live:
        for idx in list(live):
            try:
                next(gens[idx])
            except StopIteration as stop:
                results[idx] = stop.value
                live.remove(idx)
    return results


def _sb_pages(qt_ref, k_refs, v_refs, lead, tri_t, c, acc, mask):
    z = None
    for h in range(SB_HEADS):
        d = jnp.dot(_head_rows(k_refs, h, lead), qt_ref[0, h * 128:(h + 1) * 128, :],
                    preferred_element_type=F32)
        z = d if z is None else z + d
    yield
    z = z * (SB_HEAD_DIM ** -0.5)
    lb = _log_sigmoid(z)
    lk = lb - z
    if mask is not None:
        lk = jnp.where(mask, lk, 0.0)
    hi, lo = _split_bf16(lk)
    yield
    later = []
    for k in range(len(k_refs)):
        rows = slice(k * PAGE_SIZE, (k + 1) * PAGE_SIZE)
        r = (jnp.dot(tri_t, hi[rows], preferred_element_type=F32)
             + jnp.dot(tri_t, lo[rows], preferred_element_type=F32))
        later.append(r[:PAGE_SIZE] + c[0:1, :])
        c = c + r[PAGE_SIZE:]
    yield
    later = later[0] if len(later) == 1 else jnp.concatenate(later, axis=0)
    w = jnp.exp(lb + later)
    if mask is not None:
        w = jnp.where(mask, w, 0.0)
    wt = w.T
    yield
    parts = []
    for h in range(SB_HEADS):
        rows = slice(h * SAMPLE_ROWS, (h + 1) * SAMPLE_ROWS)
        parts.append(acc[rows] + jnp.dot(wt[rows].astype(BF16), _head_rows(v_refs, h, lead),
                                         preferred_element_type=F32))
    return c, jnp.concatenate(parts, axis=0)


def _da_pages(qs_ref, kt_refs, v_refs, lead, m, l, acc, mask):
    scale = DA_SUB_DIM ** -0.5
    nr = 2 * SAMPLE_ROWS
    scores = []
    for h in range(DA_HEADS):
        hs = slice(h * 128, (h + 1) * 128)
        kts = [(ref[0, hs, :] if lead else ref[hs, :]).astype(BF16) for ref in kt_refs]
        kt = kts[0] if len(kts) == 1 else jnp.concatenate(kts, axis=1)
        scores.append(jnp.dot(qs_ref[0, h], kt, preferred_element_type=F32) * scale)
    yield
    ms, ls, ps, alphas = [], [], [], []
    for h in range(DA_HEADS):
        rows = slice(h * nr, (h + 1) * nr)
        s = scores[h]
        if mask is not None:
            s = jnp.where(mask, s, -jnp.inf)
        m_old = m[rows]
        m_new = jnp.maximum(m_old, jnp.max(s, axis=1, keepdims=True))
        p = jnp.exp(s - m_new[:, :1])
        alphas.append(jnp.exp(m_old - m_new))
        ms.append(m_new)
        ls.append(alphas[h] * l[rows] + jnp.sum(p, axis=1, keepdims=True))
        ps.append(p.astype(BF16))
    yield
    accs = []
    for h in range(DA_HEADS):
        rows = slice(h * nr, (h + 1) * nr)
        accs.append(alphas[h] * acc[rows] + jnp.dot(ps[h], _head_rows(v_refs, h, lead),
                                                    preferred_element_type=F32))
    return (jnp.concatenate(ms, axis=0), jnp.concatenate(ls, axis=0), jnp.concatenate(accs, axis=0))


def _sample_attn_kernel(pt_ref, qt_ref, qs_ref, nk_ref, nv_ref, dnkt_ref, dnv_ref, *rest,
                        n_new, lambda_init):
    del pt_ref
    npp = PAGES_PER_STEP
    cache_refs = rest[:4 * npp]
    tri_ref, lq1, lk1, lq2, lk2, g_ref, oa_ref, oc_ref, c_s, acc_s, m_s, l_s, dacc_s = rest[4 * npp:]
    s = pl.program_id(1)
    tri_t = tri_ref[...]
    nr = 2 * SAMPLE_ROWS

    @pl.when(s == 0)
    def _():
        key = lax.broadcasted_iota(jnp.int32, (PAGE_SIZE, 128), 0)
        qry = lax.broadcasted_iota(jnp.int32, (PAGE_SIZE, 128), 1) % SAMPLE_ROWS
        sb_mask = (key < n_new) & (key < qry)
        qry = lax.broadcasted_iota(jnp.int32, (nr, PAGE_SIZE), 0) % SAMPLE_ROWS
        key = lax.broadcasted_iota(jnp.int32, (nr, PAGE_SIZE), 1)
        da_mask = (key < n_new) & (key <= qry)
        (c, acc), (m, l, dacc) = _run_staged(
            _sb_pages(qt_ref, [nk_ref], [nv_ref], True, tri_t,
                      jnp.zeros((SAMPLE_ROWS, 128), F32),
                      jnp.zeros((SB_HEADS * SAMPLE_ROWS, 128), F32), sb_mask),
            _da_pages(qs_ref, [dnkt_ref], [dnv_ref], True,
                      jnp.full((DA_HEADS * nr, 128), -jnp.inf, F32),
                      jnp.zeros((DA_HEADS * nr, 128), F32),
                      jnp.zeros((DA_HEADS * nr, 128), F32), da_mask))
        c_s[...] = c
        acc_s[...] = acc
        m_s[...] = m
        l_s[...] = l
        dacc_s[...] = dacc

    c, acc = c_s[...], acc_s[...]
    m, l, dacc = m_s[...], l_s[...], dacc_s[...]
    (c, acc), (m, l, dacc) = _run_staged(
        _sb_pages(qt_ref, cache_refs[0::4], cache_refs[1::4], False, tri_t, c, acc, None),
        _da_pages(qs_ref, cache_refs[2::4], cache_refs[3::4], False, m, l, dacc, None))
    c_s[...] = c
    acc_s[...] = acc
    m_s[...] = m
    l_s[...] = l
    dacc_s[...] = dacc

    @pl.when(s == pl.num_programs(1) - 1)
    def _():
        oa_ref[0] = acc
        for h in range(DA_HEADS):
            lam = _da_lambda(lq1, lk1, lq2, lk2, h, lambda_init)
            r0 = slice(h * nr, h * nr + SAMPLE_ROWS)
            r1 = slice(h * nr + SAMPLE_ROWS, (h + 1) * nr)
            oc_ref[0, h] = _da_finish((None, l[r0], dacc[r0]), (None, l[r1], dacc[r1]),
                                      lam, g_ref[...], lambda_init)


def _sample_attn(proj, caches, layer, page_table, lp, nb, ts, lambda_init):
    n_pages = page_table.shape[1]
    npp = PAGES_PER_STEP
    assert n_pages % npp == 0 and ts <= SAMPLE_ROWS
    pad_q = ((0, 0), (0, SAMPLE_ROWS - ts), (0, 0), (0, 0))

    q = jnp.pad(proj[0].reshape(nb, ts, SB_HEADS, SB_HEAD_DIM), pad_q)
    qt = jnp.einsum('bihd,hg->bhdgi', q, jnp.eye(SB_HEADS, dtype=F32))
    qt = qt.reshape(nb, SB_HEADS * SB_HEAD_DIM, SB_HEADS * SAMPLE_ROWS)
    qt = jnp.pad(qt, ((0, 0), (0, 0), (0, 128 - SB_HEADS * SAMPLE_ROWS))).astype(BF16)
    dq = jnp.pad(proj[4].reshape(nb, ts, DA_HEADS, 128), pad_q).transpose(0, 2, 1, 3)
    comp = (jnp.arange(128) // DA_SUB_DIM)[None, :] == jnp.arange(2)[:, None]
    qs = jnp.where(comp[None, None, :, None, :], dq[:, :, None], 0.0)
    qs = qs.reshape(nb, DA_HEADS, 2 * SAMPLE_ROWS, 128).astype(BF16)

    def new_page(g):
        return jnp.pad(proj[g].reshape(nb, ts * SB_HEADS, 128),
                       ((0, 0), (0, PAGE_ROWS - ts * SB_HEADS), (0, 0)))

    dnkt = jnp.pad(proj[5].reshape(nb, ts, 512).transpose(0, 2, 1),
                   ((0, 0), (0, 0), (0, PAGE_SIZE - ts)))

    kern = functools.partial(_sample_attn_kernel, n_new=ts, lambda_init=lambda_init)
    per_seq = lambda b, s, pt: (b, 0, 0)
    page_spec = pl.BlockSpec((1, PAGE_ROWS, 128), per_seq)

    def cache_spec(k):
        return pl.BlockSpec((None, None, PAGE_ROWS, 128),
                            lambda b, s, pt: (layer, pt[b, n_pages - 1 - (s * npp + k)], 0, 0))

    const2 = lambda b, s, pt: (0, 0)
    lam_spec = pl.BlockSpec((DA_HEADS, DA_SUB_DIM), const2)
    cache_specs, cache_args = [], []
    for k in range(npp):
        cache_specs += [cache_spec(k)] * 4
        cache_args += list(caches)
    n_state = DA_HEADS * 2 * SAMPLE_ROWS
    grid_spec = pltpu.PrefetchScalarGridSpec(
        num_scalar_prefetch=1,
        grid=(nb, n_pages // npp),
        in_specs=[pl.BlockSpec((1, SB_HEADS * SB_HEAD_DIM, 128), per_seq),
                  pl.BlockSpec((1, DA_HEADS, 2 * SAMPLE_ROWS, 128), lambda b, s, pt: (b, 0, 0, 0)),
                  page_spec, page_spec, page_spec, page_spec]
                 + cache_specs
                 + [pl.BlockSpec((PAGE_SIZE + 8, PAGE_SIZE), const2),
                    lam_spec, lam_spec, lam_spec, lam_spec,
                    pl.BlockSpec((1, DA_V_DIM), const2)],
        out_specs=[pl.BlockSpec((1, SB_HEADS * SAMPLE_ROWS, 128), per_seq),
                   pl.BlockSpec((1, DA_HEADS, SAMPLE_ROWS, 128), lambda b, s, pt: (b, 0, 0, 0))],
        scratch_shapes=[pltpu.VMEM((SAMPLE_ROWS, 128), F32),
                        pltpu.VMEM((SB_HEADS * SAMPLE_ROWS, 128), F32),
                        pltpu.VMEM((n_state, 128), F32), pltpu.VMEM((n_state, 128), F32),
                        pltpu.VMEM((n_state, 128), F32)],
    )
    oa, oc = pl.pallas_call(
        kern, grid_spec=grid_spec,
        out_shape=[jax.ShapeDtypeStruct((nb, SB_HEADS * SAMPLE_ROWS, 128), F32),
                   jax.ShapeDtypeStruct((nb, DA_HEADS, SAMPLE_ROWS, 128), F32)],
        compiler_params=_params("arbitrary", "arbitrary"),
        name="sample_attn",
    )(page_table, qt, qs, new_page(1), new_page(2), dnkt, new_page(6), *cache_args,
      _tri_t_aug(), lp['da_lam_q1'], lp['da_lam_k1'], lp['da_lam_q2'], lp['da_lam_k2'],
      lp['da_norm_g'].reshape(1, DA_V_DIM))
    a_out = oa.reshape(nb, SB_HEADS, SAMPLE_ROWS, 128)[:, :, :ts].transpose(0, 2, 1, 3)
    c_out = oc[:, :, :ts].transpose(0, 2, 1, 3)
    return a_out.reshape(nb * ts, BRANCH_WIDTH), c_out.reshape(nb * ts, BRANCH_WIDTH)


LANE_GROUP = 512


def _s5_kernel(u_ref, bb_ref, a_ref, h0_ref, cc_ref, d_ref, glu_ref, y_ref, hf_ref,
               bu_s, st_s, *, nseq, steps, emit):
    i = pl.program_id(0)
    n = SSM_LANES

    @pl.when(i == 0)
    def _():
        st_s[...] = h0_ref[...]

    u = u_ref[...]
    bu_s[...] = jnp.dot(u.astype(BF16), bb_ref[...], preferred_element_type=F32)

    for sg in range(nseq // 8):
        for lg in range(n // LANE_GROUP):
            lre = slice(lg * LANE_GROUP, (lg + 1) * LANE_GROUP)
            lim = slice(n + lg * LANE_GROUP, n + (lg + 1) * LANE_GROUP)
            rs = slice(sg * 8, (sg + 1) * 8)
            ar = jnp.broadcast_to(a_ref[0:1, lre], (8, LANE_GROUP))
            ai = jnp.broadcast_to(a_ref[0:1, lim], (8, LANE_GROUP))

            def step(t, carry, lre=lre, lim=lim, sg=sg, ar=ar, ai=ai):
                hr, hi = carry
                r0 = pl.multiple_of(t * nseq + sg * 8, 8)
                nr = ar * hr - ai * hi + bu_s[pl.ds(r0, 8), lre]
                ni = ar * hi + ai * hr + bu_s[pl.ds(r0, 8), lim]
                if emit:
                    bu_s[pl.ds(r0, 8), lre] = nr
                    bu_s[pl.ds(r0, 8), lim] = ni
                return nr, ni

            hr, hi = lax.fori_loop(0, steps, step, (st_s[rs, lre], st_s[rs, lim]))
            st_s[rs, lre] = hr
            st_s[rs, lim] = hi

    if emit:
        y = jnp.dot(bu_s[...].astype(BF16), cc_ref[...], preferred_element_type=F32) + d_ref[...] * u
        y = _gelu_tanh(y)
        gate = jnp.dot(y.astype(BF16), glu_ref[...], preferred_element_type=F32)
        y_ref[...] = y * _sigmoid(gate)
    else:
        y_ref[...] = jnp.zeros(y_ref.shape, F32)

    hf_ref[...] = st_s[...]


def _s5_scan(u_rows, ssm, h0, nseq, steps, emit):
    rows = u_rows.shape[0]
    blk_rows = steps * nseq
    n_chunks = rows // blk_rows
    kern = functools.partial(_s5_kernel, nseq=nseq, steps=steps, emit=emit)
    const = lambda i: (0, 0)
    y_rows = blk_rows if emit else 8
    y, hf = pl.pallas_call(
        kern,
        grid=(n_chunks,),
        in_specs=[
            pl.BlockSpec((blk_rows, SSM_WIDTH), lambda i: (i, 0)),
            pl.BlockSpec((SSM_WIDTH, 2 * SSM_LANES), const),
            pl.BlockSpec((1, 2 * SSM_LANES), const),
            pl.BlockSpec((nseq, 2 * SSM_LANES), const),
            pl.BlockSpec((2 * SSM_LANES, SSM_WIDTH), const),
            pl.BlockSpec((1, SSM_WIDTH), const),
            pl.BlockSpec((SSM_WIDTH, SSM_WIDTH), const),
        ],
        out_specs=[pl.BlockSpec((y_rows, SSM_WIDTH), (lambda i: (i, 0)) if emit else const),
                   pl.BlockSpec((nseq, 2 * SSM_LANES), const)],
        out_shape=[jax.ShapeDtypeStruct((rows if emit else 8, SSM_WIDTH), F32),
                   jax.ShapeDtypeStruct((nseq, 2 * SSM_LANES), F32)],
        scratch_shapes=[pltpu.VMEM((blk_rows, 2 * SSM_LANES), F32),
                        pltpu.VMEM((nseq, 2 * SSM_LANES), F32)],
        compiler_params=_params("arbitrary"),
        name="s5_emit" if emit else "s5_state",
    )(u_rows, ssm['bb'], ssm['a'], h0, ssm['cc'], ssm['d'], ssm['glu'])
    return y, hf


def _segment_init_kernel(a_ref, f_ref, o_ref, *, seg_len, nb, nseg):
    n = SSM_LANES
    pr = a_ref[:, :n]
    pi = a_ref[:, n:]
    rr, ri = None, None
    e = seg_len
    while e:
        if e & 1:
            if rr is None:
                rr, ri = pr, pi
            else:
                rr, ri = rr * pr - ri * pi, rr * pi + ri * pr
        e >>= 1
        if e:
            pr, pi = pr * pr - pi * pi, 2.0 * pr * pi
    for b in range(nb):
        hr = jnp.zeros((1, n), F32)
        hi = jnp.zeros((1, n), F32)
        for k in range(nseg):
            r = b * nseg + k
            o_ref[r:r + 1, :n] = hr
            o_ref[r:r + 1, n:] = hi
            fr = f_ref[r:r + 1, :n]
            fi = f_ref[r:r + 1, n:]
            hr, hi = rr * hr - ri * hi + fr, rr * hi + ri * hr + fi


def _segment_init(a, f, seg_len, nb, nseg):
    kern = functools.partial(_segment_init_kernel, seg_len=seg_len, nb=nb, nseg=nseg)
    return pl.pallas_call(
        kern, out_shape=jax.ShapeDtypeStruct(f.shape, F32), name="s5_segments",
        compiler_params=pltpu.CompilerParams(vmem_limit_bytes=VMEM_LIMIT),
    )(a, f)


def _ssm_params(lp):
    lr = lp['ssm_lam_re']
    li = lp['ssm_lam_im']
    dt = jnp.exp(lp['ssm_log_dt'])[:, None]
    mag = jnp.exp(lr * dt)
    ab_re = mag * jnp.cos(li * dt)
    ab_im = mag * jnp.sin(li * dt)
    den = lr * lr + li * li
    f_re = ((ab_re - 1.0) * lr + ab_im * li) / den
    f_im = (ab_im * lr - (ab_re - 1.0) * li) / den
    b_re = lp['ssm_b_re']
    b_im = lp['ssm_b_im']
    bb_re = f_re[..., None] * b_re - f_im[..., None] * b_im
    bb_im = f_re[..., None] * b_im + f_im[..., None] * b_re
    eye = jnp.eye(SSM_GROUPS, dtype=F32)

    def in_blockdiag(w):
        return jnp.einsum('gnp,gh->gphn', w, eye).reshape(SSM_WIDTH, SSM_LANES)

    def out_blockdiag(w):
        return jnp.einsum('gpn,gh->gnhp', w, eye).reshape(SSM_LANES, SSM_WIDTH)

    return {
        'bb': jnp.concatenate([in_blockdiag(bb_re), in_blockdiag(bb_im)], axis=1).astype(BF16),
        'cc': jnp.concatenate([out_blockdiag(lp['ssm_c_re']), -out_blockdiag(lp['ssm_c_im'])],
                              axis=0).astype(BF16),
        'a': jnp.concatenate([ab_re.reshape(1, SSM_LANES), ab_im.reshape(1, SSM_LANES)], axis=1),
        'd': lp['ssm_d'].reshape(1, SSM_WIDTH),
        'glu': lp['ssm_w_glu'].astype(BF16),
    }


def _s5_prompt(u, ssm, bsz, t, nseg, steps):
    nseq = bsz * nseg
    seg_len = t // nseg
    u_rows = u.reshape(bsz, nseg, seg_len, SSM_WIDTH).transpose(2, 0, 1, 3).reshape(seg_len * nseq, SSM_WIDTH)
    zero = jnp.zeros((nseq, 2 * SSM_LANES), F32)
    _, f = _s5_scan(u_rows, ssm, zero, nseq, steps, emit=False)
    h0 = _segment_init(ssm['a'], f, seg_len, bsz, nseg)
    y_rows, hf = _s5_scan(u_rows, ssm, h0, nseq, steps, emit=True)
    y = y_rows.reshape(seg_len, bsz, nseg, SSM_WIDTH).transpose(1, 2, 0, 3).reshape(bsz * t, SSM_WIDTH)
    last = hf.reshape(bsz, nseg, 2 * SSM_LANES)[:, -1]
    return y, last[:, :SSM_LANES], last[:, SSM_LANES:]


def _s5_sample(u, ssm, h0_re, h0_im, nb, t):
    u_rows = u.reshape(nb, t, SSM_WIDTH).transpose(1, 0, 2).reshape(t * nb, SSM_WIDTH)
    h0 = jnp.concatenate([h0_re.reshape(nb, SSM_LANES), h0_im.reshape(nb, SSM_LANES)], axis=1)
    y_rows, hf = _s5_scan(u_rows, ssm, h0, nb, t, emit=True)
    y = y_rows.reshape(t, nb, SSM_WIDTH).transpose(1, 0, 2).reshape(nb * t, SSM_WIDTH)
    return y, hf[:, :SSM_LANES], hf[:, SSM_LANES:]


def _merge_kernel(x_ref, a_ref, b_ref, c_ref, wg0, wg1, wg2, wb_ref, o_ref):
    x = x_ref[...]
    acc = None
    for n, (br, wg) in enumerate(((a_ref, wg0), (b_ref, wg1), (c_ref, wg2))):
        gate = _sigmoid(jnp.dot(x, wg[...], preferred_element_type=F32))
        pb = jnp.dot(br[...].astype(BF16), wb_ref[n], preferred_element_type=F32)
        acc = gate * pb if acc is None else acc + gate * pb
    o_ref[...] = acc.astype(o_ref.dtype)


def _merge(x_bf, a_out, b_out, c_out, w_in_bf, w_branch_bf, tm, tn):
    m = x_bf.shape[0]
    ncol = D_MODEL // tn
    g0 = GATE_COL0 // tn

    def gate_spec(n):
        return pl.BlockSpec((D_MODEL, tn), lambda i, j: (0, g0 + n * ncol + j))

    br_spec = pl.BlockSpec((tm, BRANCH_WIDTH), lambda i, j: (i, 0))
    return pl.pallas_call(
        _merge_kernel,
        grid=(m // tm, ncol),
        in_specs=[pl.BlockSpec((tm, D_MODEL), lambda i, j: (i, 0)), br_spec, br_spec, br_spec,
                  gate_spec(0), gate_spec(1), gate_spec(2),
                  pl.BlockSpec((N_BRANCH, BRANCH_WIDTH, tn), lambda i, j: (0, 0, j))],
        out_specs=pl.BlockSpec((tm, tn), lambda i, j: (i, j)),
        out_shape=jax.ShapeDtypeStruct((m, D_MODEL), BF16),
        compiler_params=_params("arbitrary", "arbitrary"),
        name="merge",
    )(x_bf, a_out, b_out, c_out, w_in_bf, w_in_bf, w_in_bf, w_branch_bf)


def _layer_norm(v, g, b):
    mu = jnp.mean(v, axis=-1, keepdims=True)
    d = v - mu
    var = jnp.mean(d * d, axis=-1, keepdims=True)
    return d * lax.rsqrt(var + LN_EPS) * g + b


def _out_ln_kernel(m_ref, w_ref, x_ref, g_ref, b_ref, h_ref, hb_ref, *, alpha):
    mix = jnp.dot(m_ref[...], w_ref[...], preferred_element_type=F32)
    h = _layer_norm(alpha * x_ref[...] + mix, g_ref[...], b_ref[...])
    h_ref[...] = h
    hb_ref[...] = h.astype(BF16)


def _out_ln(merged_bf, w_out_bf, x, g, b, alpha, tm):
    m = x.shape[0]
    row = pl.BlockSpec((tm, D_MODEL), lambda i: (i, 0))
    vec = pl.BlockSpec((1, D_MODEL), lambda i: (0, 0))
    return pl.pallas_call(
        functools.partial(_out_ln_kernel, alpha=alpha),
        grid=(m // tm,),
        in_specs=[row, pl.BlockSpec((D_MODEL, D_MODEL), lambda i: (0, 0)), row, vec, vec],
        out_specs=[row, row],
        out_shape=[jax.ShapeDtypeStruct((m, D_MODEL), F32), jax.ShapeDtypeStruct((m, D_MODEL), BF16)],
        compiler_params=_params("arbitrary"),
        name="out_ln",
    )(merged_bf, w_out_bf, x, g.reshape(1, D_MODEL), b.reshape(1, D_MODEL))


def _staircase_pairs(k):
    return [(a, b) for a in range(k) for b in range(k) if (a + 1) * (b + 1) <= k]


def _kth_largest(vals, k):
    neg = -jnp.inf
    cur = functools.reduce(jnp.maximum, vals)
    for _ in range(k - 1):
        nxt = None
        for v in vals:
            c = jnp.where(v < cur, v, neg)
            nxt = c if nxt is None else jnp.maximum(nxt, c)
        cur = nxt
    return cur


NOT_RANKED = 127.0


def _peer_route_kernel(h_ref, wq_ref, keys_ref, cnt0_ref, rank1_ref, e0_ref, e1_ref,
                       q_s, s0_s, s1_s, top_s, *, tm):
    q_s[...] = jnp.dot(h_ref[...], wq_ref[...], preferred_element_type=F32).astype(BF16)
    neg = -jnp.inf
    for hd in range(PEER_HEADS):
        for c in range(2):
            col = (hd * 2 + c) * PEER_HALF
            s = lax.dot_general(keys_ref[hd, c], q_s[:, col:col + PEER_HALF], NT_DIMS,
                                preferred_element_type=F32)
            (s0_s if c == 0 else s1_s)[hd] = s
            cur = jnp.max(s, axis=0, keepdims=True)
            top_s[c, 0, hd:hd + 1, :] = cur
            for a in range(1, PEER_TOPK):
                cur = jnp.max(jnp.where(s < cur, s, neg), axis=0, keepdims=True)
                top_s[c, a, hd:hd + 1, :] = cur
    pairs = _staircase_pairs(PEER_TOPK)
    sums = [top_s[0, a] + top_s[1, b] for a, b in pairs]
    tau = _kth_largest(sums, PEER_TOPK)
    best = top_s[0, 0] + top_s[1, 0]
    z = None
    partners = [None] * PEER_TOPK
    for (a, b), v in zip(pairs, sums):
        sel = v >= tau
        e = jnp.where(sel, jnp.exp(v - best), 0.0)
        z = e if z is None else z + e
        one = jnp.where(sel, 1.0, 0.0)
        partners[a] = one if partners[a] is None else partners[a] + one
    inv_z = 1.0 / z
    for hd in range(PEER_HEADS):
        row = slice(hd, hd + 1)
        s0 = s0_s[hd]
        s1 = s1_s[hd]
        cnt0 = jnp.zeros(s0.shape, F32)
        rank1 = jnp.full(s1.shape, NOT_RANKED, F32)
        for a in range(PEER_TOPK):
            cnt0 = jnp.where(s0 == top_s[0, a, row, :], partners[a][row, :], cnt0)
            rank1 = jnp.where(s1 == top_s[1, a, row, :], float(a), rank1)
        cnt0_ref[hd] = cnt0
        rank1_ref[hd] = rank1.astype(BF16)
        e0_ref[hd] = jnp.exp(s0 - top_s[0, 0, row, :]) * inv_z[row, :]
        e1_ref[hd] = jnp.exp(s1 - top_s[1, 0, row, :]).astype(BF16)


def _peer_route(h_bf, wq_bf, keys_bf, tm):
    m = h_bf.shape[0]
    tab = pl.BlockSpec((PEER_HEADS, PEER_KEYS, tm), lambda i: (0, 0, i))
    tab_shape = jax.ShapeDtypeStruct((PEER_HEADS, PEER_KEYS, m), F32)
    tab_shape_bf = jax.ShapeDtypeStruct((PEER_HEADS, PEER_KEYS, m), BF16)
    return pl.pallas_call(
        functools.partial(_peer_route_kernel, tm=tm),
        grid=(m // tm,),
        in_specs=[pl.BlockSpec((tm, D_MODEL), lambda i: (i, 0)),
                  pl.BlockSpec((D_MODEL, D_MODEL), lambda i: (0, 0)),
                  pl.BlockSpec((PEER_HEADS, 2, PEER_KEYS, PEER_HALF), lambda i: (0, 0, 0, 0))],
        out_specs=[tab, tab, tab, tab],
        out_shape=[tab_shape, tab_shape_bf, tab_shape, tab_shape_bf],
        scratch_shapes=[pltpu.VMEM((tm, D_MODEL), BF16),
                        pltpu.VMEM((PEER_HEADS, PEER_KEYS, tm), F32),
                        pltpu.VMEM((PEER_HEADS, PEER_KEYS, tm), F32),
                        pltpu.VMEM((2, PEER_TOPK, PEER_HEADS, tm), F32)],
        compiler_params=_params("arbitrary"),
        name="peer_route",
    )(h_bf, wq_bf, keys_bf)


GATE_ROWS = 16


def _peer_gates(ht_ref, p_ref, cnt0_ref, e0_ref, rank1_ref, e1_ref, eb):
    tile = (GATE_ROWS, ht_ref.shape[1])
    zero = jnp.zeros((), BF16)
    for ii in range(eb // PEER_KEYS):
        rows_i = [(jnp.broadcast_to(cnt0_ref[ii, hd:hd + 1, :], tile).astype(BF16),
                   jnp.broadcast_to(e0_ref[ii, hd:hd + 1, :], tile).astype(BF16))
                  for hd in range(PEER_HEADS)]
        for rt in range(PEER_KEYS // GATE_ROWS):
            r = slice(rt * GATE_ROWS, (rt + 1) * GATE_ROWS)
            w = None
            for hd in range(PEER_HEADS):
                cnt0, e0 = rows_i[hd]
                sel = jnp.where(rank1_ref[hd, r, :] < cnt0, e0 * e1_ref[hd, r, :], zero)
                w = sel if w is None else w + sel
            ro = slice(ii * PEER_KEYS + rt * GATE_ROWS, ii * PEER_KEYS + (rt + 1) * GATE_ROWS)
            p_ref[ro, :] = _gelu_tanh(ht_ref[ro, :]).astype(BF16) * w


def _peer_expert_kernel(h_ref, ua_ref, ub_ref, vta_ref, vtb_ref, t0a_ref, e0a_ref, t0b_ref, e0b_ref,
                        s1_ref, e1_ref, o_ref, ht_0, ht_1, p_0, p_1, *, eb):
    g = pl.program_id(1)
    last = pl.num_programs(1) - 1

    def first_matmul(u_ref, ht_ref):
        ht_ref[...] = lax.dot_general(u_ref[...], h_ref[...], NT_DIMS, preferred_element_type=F32)

    def second_matmul(vt_ref, p_ref):
        o_ref[...] += jnp.dot(vt_ref[...], p_ref[...], preferred_element_type=F32)

    @pl.when(g == 0)
    def _():
        o_ref[...] = jnp.zeros(o_ref.shape, F32)
        ht_1[...] = jnp.zeros(ht_1.shape, F32)
        p_0[...] = jnp.zeros(p_0.shape, BF16)

    @pl.when(g < last)
    def _():
        _peer_gates(ht_1, p_1, t0a_ref, e0a_ref, s1_ref, e1_ref, eb)
        second_matmul(vta_ref, p_0)
        first_matmul(ua_ref, ht_0)

    @pl.when(g + 1 <= last)
    def _():
        _peer_gates(ht_0, p_0, t0b_ref, e0b_ref, s1_ref, e1_ref, eb)
        second_matmul(vtb_ref, p_1)
        first_matmul(ub_ref, ht_1)

    @pl.when(g == last)
    def _():
        _peer_gates(ht_1, p_1, t0a_ref, e0a_ref, s1_ref, e1_ref, eb)
        second_matmul(vta_ref, p_0)
        second_matmul(vtb_ref, p_1)


def _peer_experts(h_bf, u_bf, vt_bf, route, tb, eb):
    m = h_bf.shape[0]
    n_eb = u_bf.shape[0] // eb
    assert n_eb % 2 == 0
    hi = n_eb - 1
    cnt0, rank1, e0, e1 = route
    cnt0 = cnt0.transpose(1, 0, 2)
    e0 = e0.transpose(1, 0, 2)
    tab = pl.BlockSpec((PEER_HEADS, PEER_KEYS, tb), lambda t, g: (0, 0, t))
    key_a = pl.BlockSpec((eb // PEER_KEYS, PEER_HEADS, tb), lambda t, g: (jnp.maximum(2 * g - 1, 0), 0, t))
    key_b = pl.BlockSpec((eb // PEER_KEYS, PEER_HEADS, tb), lambda t, g: (jnp.minimum(2 * g, hi), 0, t))
    return pl.pallas_call(
        functools.partial(_peer_expert_kernel, eb=eb),
        grid=(m // tb, n_eb // 2 + 1),
        in_specs=[pl.BlockSpec((tb, D_MODEL), lambda t, g: (t, 0)),
                  pl.BlockSpec((eb, D_MODEL), lambda t, g: (jnp.minimum(2 * g, hi), 0)),
                  pl.BlockSpec((eb, D_MODEL), lambda t, g: (jnp.minimum(2 * g + 1, hi), 0)),
                  pl.BlockSpec((D_MODEL, eb), lambda t, g: (0, jnp.clip(2 * g - 2, 0, hi))),
                  pl.BlockSpec((D_MODEL, eb), lambda t, g: (0, jnp.clip(2 * g - 1, 0, hi))),
                  key_a, key_a, key_b, key_b, tab, tab],
        out_specs=pl.BlockSpec((D_MODEL, tb), lambda t, g: (0, t)),
        out_shape=jax.ShapeDtypeStruct((D_MODEL, m), F32),
        scratch_shapes=[pltpu.VMEM((eb, tb), F32), pltpu.VMEM((eb, tb), F32),
                        pltpu.VMEM((eb, tb), BF16), pltpu.VMEM((eb, tb), BF16)],
        compiler_params=_params("arbitrary", "arbitrary"),
        name="peer_experts",
    )(h_bf, u_bf, u_bf, vt_bf, vt_bf, cnt0, e0, cnt0, e0, rank1, e1)


def _ffn_ln_kernel(ft_ref, h_ref, g_ref, b_ref, y_ref, yb_ref, *, alpha):
    y = _layer_norm(alpha * h_ref[...] + ft_ref[...].T, g_ref[...], b_ref[...])
    y_ref[...] = y
    yb_ref[...] = y.astype(BF16)


def _ffn_ln(ffn_t, h, g, b, alpha, tm):
    m = h.shape[0]
    row = pl.BlockSpec((tm, D_MODEL), lambda i: (i, 0))
    vec = pl.BlockSpec((1, D_MODEL), lambda i: (0, 0))
    return pl.pallas_call(
        functools.partial(_ffn_ln_kernel, alpha=alpha),
        grid=(m // tm,),
        in_specs=[pl.BlockSpec((D_MODEL, tm), lambda i: (0, i)), row, vec, vec],
        out_specs=[row, row],
        out_shape=[jax.ShapeDtypeStruct((m, D_MODEL), F32), jax.ShapeDtypeStruct((m, D_MODEL), BF16)],
        compiler_params=_params("arbitrary"),
        name="ffn_ln",
    )(ffn_t, h, g.reshape(1, D_MODEL), b.reshape(1, D_MODEL))


def _row_tile(m, pref):
    t = min(m, pref)
    assert m % t == 0, (m, t)
    return t


def _token_mix_tail(x, x_bf, a_out, b_out, c_out, w, lp, alpha):
    m = x.shape[0]
    merged = _merge(x_bf, a_out, b_out, c_out, w['w_in'], w['w_branch'],
                    _row_tile(m, 1024), 512)
    h, h_bf = _out_ln(merged, w['w_out'], x, lp['ln1_g'], lp['ln1_b'], alpha, _row_tile(m, 256))
    route = _peer_route(h_bf, w['peer_w_q'], w['peer_keys'], _row_tile(m, 256))
    ffn_t = _peer_experts(h_bf, w['peer_u'], w['peer_vt'], route, _row_tile(m, 512), 512)
    return _ffn_ln(ffn_t, h, lp['ln2_g'], lp['ln2_b'], alpha, _row_tile(m, 256))


def _prompt_layer(x, x_bf, lp, w, ssm, rope_tabs, bsz, t, lambda_init, alpha):
    proj = _project(x_bf, w['w_in'], rope_tabs, _row_tile(bsz * t, 1024))
    a_out = _sb_prompt(proj, bsz, t, _row_tile(t, 256))
    c_out = _da_prompt(proj, lp, bsz, t, _row_tile(t, 512), lambda_init)
    nseg = 8 // bsz if 8 % bsz == 0 and t % (8 // bsz) == 0 else 1
    assert (bsz * nseg) % 8 == 0
    seg_len = t // nseg
    b_out, h_re, h_im = _s5_prompt(proj[3], ssm, bsz, t, nseg, _row_tile(seg_len, 64))
    y, y_bf = _token_mix_tail(x, x_bf, a_out, b_out, c_out, w, lp, alpha)
    state = (proj[1], proj[2], proj[5], proj[6], h_re, h_im)
    return y, y_bf, state


def _sample_layer(x, x_bf, lp, w, ssm, rope_tabs, caches, layer, page_table, h0_re, h0_im,
                  nb, t, lambda_init, alpha):
    proj = _project(x_bf, w['w_in'], rope_tabs, nb * t)
    a_out, c_out = _sample_attn(proj, caches, layer, page_table, lp, nb, t, lambda_init)
    b_out, h_re, h_im = _s5_sample(proj[3], ssm, h0_re, h0_im, nb, t)
    y, y_bf = _token_mix_tail(x, x_bf, a_out, b_out, c_out, w, lp, alpha)
    state = (proj[1], proj[2], proj[5], proj[6], h_re, h_im)
    return y, y_bf, state


def kernel(x_prompt, x_sample, cache_sb_k, cache_sb_v, cache_da_k, cache_da_v, state_ssm_re, state_ssm_im, page_table, w_in, w_branch, w_out, ssm_lam_re, ssm_lam_im, ssm_log_dt, ssm_b_re, ssm_b_im, ssm_c_re, ssm_c_im, ssm_d, ssm_w_glu, da_lam_q1, da_lam_k1, da_lam_q2, da_lam_k2, da_norm_g, ln1_g, ln1_b, peer_w_q, peer_keys, peer_u, peer_v, ln2_g, ln2_b):
    depth = w_in.shape[0]
    bsz, t, _ = x_prompt.shape
    nb, ts, _ = x_sample.shape
    n_past = page_table.shape[1] * PAGE_SIZE
    alpha = (2 * depth) ** 0.25
    n_pool = cache_sb_k.shape[1]
    caches = (cache_sb_k.reshape(depth, n_pool, PAGE_ROWS, 128),
              cache_sb_v.reshape(depth, n_pool, PAGE_ROWS, 128),
              jnp.transpose(cache_da_k, (0, 1, 3, 4, 5, 2)).reshape(depth, n_pool, PAGE_ROWS, PAGE_SIZE),
              cache_da_v.reshape(depth, n_pool, PAGE_ROWS, 128))
    rope_p = _rope_tables(jnp.tile(jnp.arange(t, dtype=jnp.int32), bsz))
    rope_s = _rope_tables(jnp.tile(n_past + jnp.arange(ts, dtype=jnp.int32), nb))

    yp = x_prompt.reshape(bsz * t, D_MODEL)
    ys = x_sample.reshape(nb * ts, D_MODEL)
    yp_bf = yp.astype(BF16)
    ys_bf = ys.astype(BF16)
    rows_p, rows_s = [], []
    for l in range(depth):
        lp = {'ssm_lam_re': ssm_lam_re[l], 'ssm_lam_im': ssm_lam_im[l], 'ssm_log_dt': ssm_log_dt[l],
              'ssm_b_re': ssm_b_re[l], 'ssm_b_im': ssm_b_im[l],
              'ssm_c_re': ssm_c_re[l], 'ssm_c_im': ssm_c_im[l],
              'ssm_d': ssm_d[l], 'ssm_w_glu': ssm_w_glu[l],
              'da_lam_q1': da_lam_q1[l], 'da_lam_k1': da_lam_k1[l],
              'da_lam_q2': da_lam_q2[l], 'da_lam_k2': da_lam_k2[l], 'da_norm_g': da_norm_g[l],
              'ln1_g': ln1_g[l], 'ln1_b': ln1_b[l], 'ln2_g': ln2_g[l], 'ln2_b': ln2_b[l]}
        w = {'w_in': w_in[l].astype(BF16), 'w_branch': w_branch[l].astype(BF16),
             'w_out': w_out[l].astype(BF16), 'peer_w_q': peer_w_q[l].astype(BF16),
             'peer_keys': peer_keys[l].astype(BF16), 'peer_u': peer_u[l].astype(BF16),
             'peer_vt': peer_v[l].astype(BF16).T}
        ssm = _ssm_params(lp)
        lambda_init = 0.8 - 0.6 * math.exp(-0.3 * l)
        yp, yp_bf, st_p = _prompt_layer(yp, yp_bf, lp, w, ssm, rope_p, bsz, t, lambda_init, alpha)
        ys, ys_bf, st_s = _sample_layer(ys, ys_bf, lp, w, ssm, rope_s, caches, l, page_table,
                                        state_ssm_re[l], state_ssm_im[l], nb, ts, lambda_init, alpha)
        rows_p.append(st_p)
        rows_s.append(st_s)

    def pack(rows, lead):
        sk, sv, dk, dv, hr, hi = [jnp.stack(z) for z in zip(*rows)]
        n = lead[0]
        return (sk.reshape(depth, *lead, SB_HEADS, SB_HEAD_DIM),
                sv.reshape(depth, *lead, SB_HEADS, SB_HEAD_DIM),
                dk.reshape(depth, *lead, DA_HEADS, 2, DA_SUB_DIM),
                dv.reshape(depth, *lead, DA_HEADS, DA_V_DIM),
                hr.reshape(depth, n, SSM_GROUPS, SSM_STATE),
                hi.reshape(depth, n, SSM_GROUPS, SSM_STATE))

    return ((yp.reshape(bsz, t, D_MODEL), ys.reshape(nb, ts, D_MODEL))
            + pack(rows_p, (bsz, t)) + pack(rows_s, (nb, ts)))
```
